```python
import math
import jax, jax.numpy as jnp
from jax import lax
import numpy as np

D_MODEL = 4096
BATCH = 16
SEQ = 256
DEPTH = 2
DEC_BATCH = 2
DEC_SEQ = 1024
PAST_LEN = 512

GRID_W = 64
N_MIXERS = 2
N_ATTN_LAYERS = (DEPTH + 1) // 2
N_RWKV_LAYERS = DEPTH // 2
N_HEADS = 32
N_KV_HEADS = 8
HEAD_DIM = D_MODEL // N_HEADS
Q_BLOCK = 128
ROPE_THETA = 10000.0
RWKV_HEAD = 64
RWKV_HEADS = D_MODEL // RWKV_HEAD
DECAY_LORA = 128
AAA_LORA = 128
GATE_LORA = 480
N_EXPERTS = 64
TOP_K = 8
EXPERT_FF = 1024
SHARED_FF = 1024
ROUTED_SCALE = 2.5
MOE_BLOCK = 128
N_MOD = 6
NORM_EPS = 1e-6
GN_EPS = 64e-5

kernel_name = "hybrid_attn_rwkv7_moe_diffusion_step"

F32 = jnp.float32


def rms_norm(x, gain):
    xf = x.astype(F32)
    y = xf * lax.rsqrt(jnp.mean(xf * xf, axis=-1, keepdims=True) + NORM_EPS)
    return (y * gain.astype(F32)).astype(x.dtype)


def ada_chunks(cvec, w_ada, b_ada, i):
    m = jax.nn.silu(cvec) @ w_ada[i] + b_ada[i]
    return [t[:, None, :] for t in jnp.split(m, N_MOD, axis=-1)]


def modulate(x, gain, shift, scale):
    return rms_norm(x, gain) * (1 + scale) + shift


def axial_rope_tables(n_tokens):
    rows = n_tokens // GRID_W
    r_idx, c_idx = jnp.meshgrid(jnp.arange(rows), jnp.arange(GRID_W), indexing="ij")
    r_idx = r_idx.reshape(-1).astype(F32)
    c_idx = c_idx.reshape(-1).astype(F32)
    n_freq = HEAD_DIM // 4
    inv = ROPE_THETA ** (-jnp.arange(n_freq, dtype=F32) / n_freq)
    ang = jnp.concatenate([r_idx[:, None] * inv, c_idx[:, None] * inv], axis=-1)
    return jnp.cos(ang), jnp.sin(ang)


def apply_rope(x, cos, sin):
    xf = x.astype(F32)
    half = HEAD_DIM // 2
    x1, x2 = xf[..., :half], xf[..., half:]
    c = cos[None, :, None, :]
    s = sin[None, :, None, :]
    return jnp.concatenate([x1 * c - x2 * s, x1 * s + x2 * c], axis=-1).astype(x.dtype)


def qkv_heads(h, w_qkv, q_gain, k_gain, j):
    b, t, _ = h.shape
    qkv = h @ w_qkv[j]
    q, k, v = jnp.split(qkv, [N_HEADS * HEAD_DIM, (N_HEADS + N_KV_HEADS) * HEAD_DIM], axis=-1)
    q = rms_norm(q.reshape(b, t, N_HEADS, HEAD_DIM), q_gain[j])
    k = rms_norm(k.reshape(b, t, N_KV_HEADS, HEAD_DIM), k_gain[j])
    v = v.reshape(b, t, N_KV_HEADS, HEAD_DIM)
    return q, k, v


def block_attention(q, k, v):
    b, lq, _, _ = q.shape
    g = N_HEADS // N_KV_HEADS
    nb = lq // Q_BLOCK
    qb = q.reshape(b, nb, Q_BLOCK, N_KV_HEADS, g, HEAD_DIM).transpose(1, 0, 2, 3, 4, 5)
    scale = HEAD_DIM ** -0.5

    def one_block(qblk):
        s = jnp.einsum("bqkgd,bskd->bkgqs", qblk, k).astype(F32) * scale
        p = jax.nn.softmax(s, axis=-1).astype(v.dtype)
        return jnp.einsum("bkgqs,bskd->bqkgd", p, v)

    o = lax.map(one_block, qb)
    return o.transpose(1, 0, 2, 3, 4, 5).reshape(b, lq, N_HEADS * HEAD_DIM)


def centred_shift_delta(h):
    zero = jnp.zeros_like(h[:, :1])
    prev = jnp.concatenate([zero, h[:, :-1]], axis=1)
    nxt = jnp.concatenate([h[:, 1:], zero], axis=1)
    return 0.5 * (prev + nxt) - h


def wkv_scan(r, w, k, v, a, bb, s0, reverse):
    def step(s, inp):
        r_t, w_t, k_t, v_t, a_t, b_t = inp
        sa = jnp.einsum("bhvk,bhk->bhv", s, a_t)
        s = s * w_t[:, :, None, :] + sa[..., None] * b_t[:, :, None, :] + v_t[..., None] * k_t[:, :, None, :]
        return s, jnp.einsum("bhvk,bhk->bhv", s, r_t)

    xs = tuple(jnp.moveaxis(t.astype(F32), 1, 0) for t in (r, w, k, v, a, bb))
    s_fin, ys = lax.scan(step, s0.astype(F32), xs, reverse=reverse)
    return jnp.moveaxis(ys, 0, 1), s_fin


def rwkv_time_mix(h, s0, j, mu, w_r, w_k, w_v, w_o, w0, w1, w2, a0, a1, a2, g1, g2, k_k, k_a, r_k, lnx_w, lnx_b):
    b, t, d = h.shape
    hs = (b, t, RWKV_HEADS, RWKV_HEAD)
    hn = (RWKV_HEADS, RWKV_HEAD)
    xx = centred_shift_delta(h)
    xr, xw, xk, xv, xa, xg = [h + xx * mu[j, n] for n in range(6)]
    r = (xr @ w_r[j]).reshape(hs)
    k = (xk @ w_k[j]).reshape(hs)
    v = (xv @ w_v[j]).reshape(hs)
    kkf = (k * k_k[j].reshape(hn)).astype(F32)
    kk = kkf / jnp.maximum(jnp.sqrt(jnp.sum(kkf * kkf, axis=-1, keepdims=True)), 1e-12)
    outs, states = [], []
    for dr in range(2):
        w = jnp.exp(-0.606531 * jax.nn.sigmoid((w0[j, dr] + jnp.tanh(xw @ w1[j, dr]) @ w2[j, dr]).astype(F32))).reshape(hs)
        a = jax.nn.sigmoid((a0[j, dr] + (xa @ a1[j, dr]) @ a2[j, dr]).astype(F32)).reshape(hs)
        kd = k.astype(F32) * (1.0 + (a - 1.0) * k_a[j].reshape(hn).astype(F32))
        y, s_fin = wkv_scan(r, w, kd, v, -kk, kk * a, s0[:, dr], reverse=(dr == 1))
        mean = jnp.mean(y, axis=-1, keepdims=True)
        var = jnp.mean(jnp.square(y - mean), axis=-1, keepdims=True)
        y = (y - mean) * lax.rsqrt(var + GN_EPS) * lnx_w[j].reshape(hn).astype(F32) + lnx_b[j].reshape(hn).astype(F32)
        bonus = jnp.sum(r.astype(F32) * kd * r_k[j, dr].astype(F32), axis=-1, keepdims=True) * v.astype(F32)
        g = jax.nn.sigmoid(xg @ g1[j, dr]) @ g2[j, dr]
        outs.append((y + bonus).reshape(b, t, d).astype(h.dtype) * g)
        states.append(s_fin)
    return (outs[0] + outs[1]) @ w_o[j], jnp.stack(states, axis=1)


def routed_experts(x, idx, gw, w_gate, w_up, w_down, i):
    n, d = x.shape
    a = n * TOP_K
    n_blocks = (a + N_EXPERTS * (MOE_BLOCK - 1) + MOE_BLOCK - 1) // MOE_BLOCK
    n_slots = n_blocks * MOE_BLOCK
    e_flat = idx.reshape(-1)
    tok_flat = jnp.arange(a, dtype=jnp.int32) // TOP_K
    order = jnp.argsort(e_flat)
    e_sorted = e_flat[order]
    counts = jnp.bincount(e_flat, length=N_EXPERTS)
    padded = (counts + MOE_BLOCK - 1) // MOE_BLOCK * MOE_BLOCK
    pad_end = jnp.cumsum(padded)
    pad_start = pad_end - padded
    start = jnp.cumsum(counts) - counts
    dest = pad_start[e_sorted] + jnp.arange(a) - start[e_sorted]
    slot_tok = jnp.full((n_slots,), n, jnp.int32).at[dest].set(tok_flat[order])
    slot_w = jnp.zeros((n_slots,), x.dtype).at[dest].set(gw.reshape(-1)[order].astype(x.dtype))
    block_e = jnp.minimum(jnp.searchsorted(pad_end, jnp.arange(n_blocks) * MOE_BLOCK, side="right"), N_EXPERTS - 1)
    x_pad = jnp.concatenate([x, jnp.zeros((1, d), x.dtype)], axis=0)

    def expert_block(args):
        toks, wts, e = args
        xb = x_pad[toks]
        hb = jax.nn.silu(xb @ w_gate[i, e]) * (xb @ w_up[i, e])
        return (hb @ w_down[i, e]) * wts[:, None]

    yb = lax.map(expert_block, (slot_tok.reshape(n_blocks, MOE_BLOCK), slot_w.reshape(n_blocks, MOE_BLOCK), block_e))
    y = jnp.zeros((n + 1, d), x.dtype).at[slot_tok].add(yb.reshape(n_slots, d))
    return y[:n]


def moe_ffn(h, i, w_router, b_router, w_gate, w_up, w_down, s_gate, s_up, s_down):
    b, t, d = h.shape
    x = h.reshape(b * t, d)
    scores = jax.nn.sigmoid((x @ w_router[i]).astype(F32))
    _, idx = lax.top_k(scores + b_router[i].astype(F32), TOP_K)
    sel = jnp.take_along_axis(scores, idx, axis=-1)
    gw = sel / jnp.sum(sel, axis=-1, keepdims=True) * ROUTED_SCALE
    routed = routed_experts(x, idx, gw, w_gate, w_up, w_down, i)
    shared = (jax.nn.silu(x @ s_gate[i]) * (x @ s_up[i])) @ s_down[i]
    return (routed + shared).reshape(b, t, d)


def setup_inputs(seed: int = 0) -> dict:
    key = jax.random.key(seed)
    keys = iter(jax.random.split(key, 64))

    def nrm(shape, scale):
        return jax.random.normal(next(keys), shape, jnp.float32) * scale

    def uni(shape):
        return jax.random.uniform(next(keys), shape, jnp.float32, 0.0, 1.0)

    D = D_MODEL
    NA, NR = N_ATTN_LAYERS, N_RWKV_LAYERS
    QKV = (N_HEADS + 2 * N_KV_HEADS) * HEAD_DIM
    return {
        "x_prompt": nrm((BATCH, SEQ, D), 1.0),
        "x_sample": nrm((DEC_BATCH, DEC_SEQ, D), 1.0),
        "c": nrm((DEC_BATCH, D), 1.0),
        "cache_k": nrm((DEC_BATCH, NA, PAST_LEN, N_KV_HEADS, HEAD_DIM), 1.0),
        "cache_v": nrm((DEC_BATCH, NA, PAST_LEN, N_KV_HEADS, HEAD_DIM), 1.0),
        "state_wkv": nrm((DEC_BATCH, NR, 2, RWKV_HEADS, RWKV_HEAD, RWKV_HEAD), 0.1),
        "c_ctx": nrm((D,), 1.0),
        "w_ada": nrm((DEPTH, D, N_MOD * D), 0.5 * D ** -0.5),
        "b_ada": nrm((DEPTH, N_MOD * D), 0.02),
        "norm_mix": 1.0 + nrm((DEPTH, D), 0.02),
        "norm_ffn": 1.0 + nrm((DEPTH, D), 0.02),
        "attn_w_qkv": nrm((NA, D, QKV), D ** -0.5),
        "attn_w_o": nrm((NA, N_HEADS * HEAD_DIM, D), (N_HEADS * HEAD_DIM) ** -0.5),
        "attn_q_gain": 1.0 + nrm((NA, HEAD_DIM), 0.02),
        "attn_k_gain": 1.0 + nrm((NA, HEAD_DIM), 0.02),
        "rw_mu": uni((NR, 6, D)),
        "rw_w_r": nrm((NR, D, D), D ** -0.5),
        "rw_w_k": nrm((NR, D, D), D ** -0.5),
        "rw_w_v": nrm((NR, D, D), D ** -0.5),
        "rw_w_o": nrm((NR, D, D), D ** -0.5),
        "rw_w0": nrm((NR, 2, D), 0.5),
        "rw_w1": nrm((NR, 2, D, DECAY_LORA), D ** -0.5),
        "rw_w2": nrm((NR, 2, DECAY_LORA, D), DECAY_LORA ** -0.5),
        "rw_a0": nrm((NR, 2, D), 0.1),
        "rw_a1": nrm((NR, 2, D, AAA_LORA), D ** -0.5),
        "rw_a2": nrm((NR, 2, AAA_LORA, D), AAA_LORA ** -0.5),
        "rw_g1": nrm((NR, 2, D, GATE_LORA), D ** -0.5),
        "rw_g2": nrm((NR, 2, GATE_LORA, D), GATE_LORA ** -0.5),
        "rw_k_k": 0.85 + nrm((NR, D), 0.02),
        "rw_k_a": 1.0 + nrm((NR, D), 0.02),
        "rw_r_k": nrm((NR, 2, RWKV_HEADS, RWKV_HEAD), 0.1),
        "rw_lnx_w": 1.0 + nrm((NR, D), 0.02),
        "rw_lnx_b": nrm((NR, D), 0.02),
        "moe_w_router": nrm((DEPTH, D, N_EXPERTS), D ** -0.5),
        "moe_b_router": nrm((DEPTH, N_EXPERTS), 0.01),
        "moe_w_gate": nrm((DEPTH, N_EXPERTS, D, EXPERT_FF), D ** -0.5),
        "moe_w_up": nrm((DEPTH, N_EXPERTS, D, EXPERT_FF), D ** -0.5),
        "moe_w_down": nrm((DEPTH, N_EXPERTS, EXPERT_FF, D), EXPERT_FF ** -0.5),
        "sh_w_gate": nrm((DEPTH, D, SHARED_FF), D ** -0.5),
        "sh_w_up": nrm((DEPTH, D, SHARED_FF), D ** -0.5),
        "sh_w_down": nrm((DEPTH, SHARED_FF, D), SHARED_FF ** -0.5),
    }


def reference(x_prompt, x_sample, c, cache_k, cache_v, state_wkv, c_ctx, w_ada, b_ada, norm_mix, norm_ffn,
              attn_w_qkv, attn_w_o, attn_q_gain, attn_k_gain,
              rw_mu, rw_w_r, rw_w_k, rw_w_v, rw_w_o, rw_w0, rw_w1, rw_w2, rw_a0, rw_a1, rw_a2, rw_g1, rw_g2,
              rw_k_k, rw_k_a, rw_r_k, rw_lnx_w, rw_lnx_b,
              moe_w_router, moe_b_router, moe_w_gate, moe_w_up, moe_w_down, sh_w_gate, sh_w_up, sh_w_down):
    rw = (rw_mu, rw_w_r, rw_w_k, rw_w_v, rw_w_o, rw_w0, rw_w1, rw_w2, rw_a0, rw_a1, rw_a2, rw_g1, rw_g2,
          rw_k_k, rw_k_a, rw_r_k, rw_lnx_w, rw_lnx_b)
    moe = (moe_w_router, moe_b_router, moe_w_gate, moe_w_up, moe_w_down, sh_w_gate, sh_w_up, sh_w_down)
    cos, sin = axial_rope_tables(x_sample.shape[1])
    xp, xs = x_prompt, x_sample
    zero_state = jnp.zeros((x_prompt.shape[0], 2, RWKV_HEADS, RWKV_HEAD, RWKV_HEAD), F32)
    new_k, new_v, new_s = [], [], []
    for i in range(DEPTH):
        j = i // N_MIXERS
        p_sh1, p_sc1, p_g1, p_sh2, p_sc2, p_g2 = ada_chunks(c_ctx[None, :], w_ada, b_ada, i)
        s_sh1, s_sc1, s_g1, s_sh2, s_sc2, s_g2 = ada_chunks(c, w_ada, b_ada, i)
        hp = modulate(xp, norm_mix[i], p_sh1, p_sc1)
        hs = modulate(xs, norm_mix[i], s_sh1, s_sc1)
        if i % N_MIXERS == 0:
            qp, kp, vp = qkv_heads(hp, attn_w_qkv, attn_q_gain, attn_k_gain, j)
            op = block_attention(qp, kp, vp) @ attn_w_o[j]
            ql, kl, vl = qkv_heads(hs, attn_w_qkv, attn_q_gain, attn_k_gain, j)
            ql = apply_rope(ql, cos, sin)
            kl = apply_rope(kl, cos, sin)
            k_all = jnp.concatenate([cache_k[:, j].astype(kl.dtype), kl], axis=1)
            v_all = jnp.concatenate([cache_v[:, j].astype(vl.dtype), vl], axis=1)
            os_ = block_attention(ql, k_all, v_all) @ attn_w_o[j]
            new_k.append(kp)
            new_v.append(vp)
        else:
            op, sp = rwkv_time_mix(hp, zero_state, j, *rw)
            os_, _ = rwkv_time_mix(hs, state_wkv[:, j], j, *rw)
            new_s.append(sp)
        xp = xp + p_g1 * op
        xs = xs + s_g1 * os_
        xp = xp + p_g2 * moe_ffn(modulate(xp, norm_ffn[i], p_sh2, p_sc2), i, *moe)
        xs = xs + s_g2 * moe_ffn(modulate(xs, norm_ffn[i], s_sh2, s_sc2), i, *moe)
    new_cache_k = jnp.stack(new_k, axis=1)
    new_cache_v = jnp.stack(new_v, axis=1)
    new_state_wkv = jnp.stack(new_s, axis=1)
    return (xp, xs, new_cache_k, new_cache_v, new_state_wkv)
```

```python
import functools

import jax
import jax.numpy as jnp
from jax import lax
from jax.experimental import pallas as pl
from jax.experimental.pallas import tpu as pltpu

D_MODEL = 4096
BATCH = 16
SEQ = 256
DEPTH = 2
DEC_BATCH = 2
DEC_SEQ = 1024
PAST_LEN = 512
GRID_W = 64
N_HEADS = 32
N_KV_HEADS = 8
HEAD_DIM = 128
ROPE_THETA = 10000.0
RWKV_HEAD = 64
DECAY_LORA = 128
AAA_LORA = 128
GATE_LORA = 480
N_EXPERTS = 64
TOP_K = 8
EXPERT_FF = 1024
SHARED_FF = 1024
ROUTED_SCALE = 2.5
N_MOD = 6
NORM_EPS = 1e-6
GN_EPS = 64e-5
DECAY_SCALE = -0.606531

F32 = jnp.float32
BF16 = jnp.bfloat16

V7X_VMEM_LIMIT_BYTES = 56 * 1024 * 1024
LANES = 128
ROW_BLOCK = 256
MM_TM = 512
MM_TN = 512
SCAN_CHUNK = 64
ATTN_TQ = 512


def _params(*sem):
    return pltpu.CompilerParams(dimension_semantics=sem, vmem_limit_bytes=V7X_VMEM_LIMIT_BYTES)


def _n_ctx():
    return BATCH * SEQ


def _n_tok():
    return BATCH * SEQ + DEC_BATCH * DEC_SEQ


def _group_of_block(i, tm):
    n_ctx_blocks = _n_ctx() // tm
    return jnp.where(i < n_ctx_blocks, 0, 1 + (i - n_ctx_blocks) // (DEC_SEQ // tm))


def _silu(x):
    return x * jax.nn.sigmoid(x)


def _ada_body(c_ref, w_ref, b_ref, o_ref):
    s = _silu(c_ref[...]).astype(BF16)
    o_ref[...] = jnp.dot(s, w_ref[...].astype(BF16), preferred_element_type=F32) + b_ref[...]


def ada_all(cvec8, w_ada, b_ada, tn=512):
    depth, d, n6 = w_ada.shape
    return pl.pallas_call(
        _ada_body,
        grid=(depth, n6 // tn),
        in_specs=[pl.BlockSpec((8, d), lambda l, j: (0, 0)),
                  pl.BlockSpec((None, d, tn), lambda l, j: (l, 0, j)),
                  pl.BlockSpec((None, 1, tn), lambda l, j: (l, 0, j))],
        out_specs=pl.BlockSpec((None, 8, tn), lambda l, j: (l, 0, j)),
        out_shape=jax.ShapeDtypeStruct((depth, 8, n6), F32),
        compiler_params=_params("arbitrary", "arbitrary"),
    )(cvec8, w_ada, b_ada.reshape(depth, 1, n6))


def _modulated(x, gain, shift, scale):
    y = x * lax.rsqrt(jnp.mean(x * x, axis=-1, keepdims=True) + NORM_EPS) * gain
    return y * (1.0 + scale) + shift


def _modulate_body(x_ref, gain_ref, sh_ref, sc_ref, o_ref):
    o_ref[...] = _modulated(x_ref[...], gain_ref[...], sh_ref[...], sc_ref[...]).astype(o_ref.dtype)


def _group_spec(d, tm):
    return pl.BlockSpec((None, 1, d), lambda i: (_group_of_block(i, tm), 0, 0))


def modulate(x, gain, shift, scale):
    n, d = x.shape
    tm = ROW_BLOCK
    return pl.pallas_call(
        _modulate_body,
        grid=(n // tm,),
        in_specs=[pl.BlockSpec((tm, d), lambda i: (i, 0)),
                  pl.BlockSpec((1, d), lambda i: (0, 0)),
                  _group_spec(d, tm), _group_spec(d, tm)],
        out_specs=pl.BlockSpec((tm, d), lambda i: (i, 0)),
        out_shape=jax.ShapeDtypeStruct((n, d), BF16),
        compiler_params=_params("arbitrary"),
    )(x, gain.reshape(1, d), shift, scale)


def _split_bf16(x):
    hi = x.astype(BF16)
    lo = (x - hi.astype(F32)).astype(BF16)
    return hi, lo


def _router_body(x_ref, gain_ref, sh_ref, sc_ref, wr_ref, o_ref, s_ref):
    h = _modulated(x_ref[...], gain_ref[...], sh_ref[...], sc_ref[...])
    o_ref[...] = h.astype(BF16)
    h_hi, h_lo = _split_bf16(h)
    w_hi, w_lo = _split_bf16(wr_ref[...])
    logits = (jnp.dot(h_hi, w_hi, preferred_element_type=F32)
              + jnp.dot(h_lo, w_hi, preferred_element_type=F32)
              + jnp.dot(h_hi, w_lo, preferred_element_type=F32))
    s_ref[...] = jax.nn.sigmoid(logits)


def modulate_and_route(x, gain, shift, scale, w_router):
    n, d = x.shape
    e = w_router.shape[1]
    tm = ROW_BLOCK
    return pl.pallas_call(
        _router_body,
        grid=(n // tm,),
        in_specs=[pl.BlockSpec((tm, d), lambda i: (i, 0)),
                  pl.BlockSpec((1, d), lambda i: (0, 0)),
                  _group_spec(d, tm), _group_spec(d, tm),
                  pl.BlockSpec((d, e), lambda i: (0, 0))],
        out_specs=[pl.BlockSpec((tm, d), lambda i: (i, 0)),
                   pl.BlockSpec((tm, e), lambda i: (i, 0))],
        out_shape=[jax.ShapeDtypeStruct((n, d), BF16), jax.ShapeDtypeStruct((n, e), F32)],
        compiler_params=_params("arbitrary"),
    )(x, gain.reshape(1, d), shift, scale, w_router)


def _mm_body(*refs, n_w, n_extra, epilogue):
    a_ref = refs[0]
    w_refs = refs[1:1 + n_w]
    extra = refs[1 + n_w:1 + n_w + n_extra]
    o_ref = refs[1 + n_w + n_extra]
    w_bf = refs[2 + n_w + n_extra:]

    @pl.when(pl.program_id(1) == 0)
    def _():
        for w_ref, w_s in zip(w_refs, w_bf):
            w_s[...] = w_ref[...].astype(BF16)

    a = a_ref[...]
    accs = [jnp.dot(a, w_s[...], preferred_element_type=F32) for w_s in w_bf]
    o_ref[...] = epilogue(*accs, *[r[...] for r in extra]).astype(o_ref.dtype)


def matmul(a, ws, epilogue=None, extra=(), out_dtype=F32, tm=None, tn=None):
    m, k = a.shape
    n = ws[0].shape[1]
    tm = min(tm or MM_TM, m)
    tn = min(tn or MM_TN, n)
    if epilogue is None:
        epilogue = lambda acc: acc
    body = functools.partial(_mm_body, n_w=len(ws), n_extra=len(extra), epilogue=epilogue)
    return pl.pallas_call(
        body,
        grid=(n // tn, m // tm),
        in_specs=([pl.BlockSpec((tm, k), lambda j, i: (i, 0))]
                  + [pl.BlockSpec((k, tn), lambda j, i: (0, j)) for _ in ws]
                  + [spec for _, spec in extra]),
        out_specs=pl.BlockSpec((tm, tn), lambda j, i: (i, j)),
        out_shape=jax.ShapeDtypeStruct((m, n), out_dtype),
        scratch_shapes=[pltpu.VMEM((k, tn), BF16) for _ in ws],
        compiler_params=_params("arbitrary", "arbitrary"),
    )(a, *ws, *[arr for arr, _ in extra])


def _row_vec_spec(tn):
    return pl.BlockSpec((1, tn), lambda j, i: (0, j))


def _gate_spec(tm, tn):
    return pl.BlockSpec((None, 1, tn), lambda j, i: (_group_of_block(i, tm), 0, j))


def _tile_spec(tm, tn):
    return pl.BlockSpec((tm, tn), lambda j, i: (i, j))


def _rms_head(x, gain):
    return x * lax.rsqrt(jnp.mean(x * x, axis=-1, keepdims=True) + NORM_EPS) * gain


def _rope(x, cos_f, sin_f):
    return x * cos_f + pltpu.roll(x, HEAD_DIM // 2, 1) * sin_f


def _attn_ctx_body(q_ref, k_ref, v_ref, qg_ref, kg_ref, o_ref, kc_ref, vc_ref):
    kn = _rms_head(k_ref[...], kg_ref[...])
    kc_ref[...] = kn
    v = v_ref[...]
    vc_ref[...] = v
    kb = kn.astype(BF16)
    vb = v.astype(BF16)
    scale = HEAD_DIM ** -0.5
    for h in range(N_HEADS // N_KV_HEADS):
        sl = slice(h * HEAD_DIM, (h + 1) * HEAD_DIM)
        qh = _rms_head(q_ref[:, sl], qg_ref[...]).astype(BF16)
        s = lax.dot_general(qh, kb, (((1,), (1,)), ((), ())), preferred_element_type=F32) * scale
        p = jnp.exp(s - jnp.max(s, axis=-1, keepdims=True))
        l = jnp.sum(p, axis=-1, keepdims=True)
        o = jnp.dot(p.astype(BF16), vb, preferred_element_type=F32) / l
        o_ref[:, sl] = o.astype(o_ref.dtype)


def attention_context(qkv, q_gain, k_gain):
    n = qkv.shape[0]
    g = N_HEADS // N_KV_HEADS
    hd = HEAD_DIM
    return pl.pallas_call(
        _attn_ctx_body,
        grid=(BATCH, N_KV_HEADS),
        in_specs=[pl.BlockSpec((SEQ, g * hd), lambda b, kh: (b, kh)),
                  pl.BlockSpec((SEQ, hd), lambda b, kh: (b, N_HEADS + kh)),
                  pl.BlockSpec((SEQ, hd), lambda b, kh: (b, N_HEADS + N_KV_HEADS + kh)),
                  pl.BlockSpec((1, hd), lambda b, kh: (0, 0)),
                  pl.BlockSpec((1, hd), lambda b, kh: (0, 0))],
        out_specs=[pl.BlockSpec((SEQ, g * hd), lambda b, kh: (b, kh)),
                   pl.BlockSpec((SEQ, hd), lambda b, kh: (b, kh)),
                   pl.BlockSpec((SEQ, hd), lambda b, kh: (b, kh))],
        out_shape=[jax.ShapeDtypeStruct((n, N_HEADS * hd), BF16),
                   jax.ShapeDtypeStruct((_n_ctx(), N_KV_HEADS * hd), F32),
                   jax.ShapeDtypeStruct((_n_ctx(), N_KV_HEADS * hd), F32)],
        compiler_params=_params("arbitrary", "arbitrary"),
    )(qkv, qkv, qkv, q_gain.reshape(1, hd), k_gain.reshape(1, hd))


def _attn_lat_body(q_ref, k_ref, v_ref, ck_ref, cv_ref, cosq_ref, sinq_ref, cosk_ref, sink_ref,
                   qg_ref, kg_ref, o_in_ref, o_ref):
    del o_in_ref
    kn = _rope(_rms_head(k_ref[...], kg_ref[...]), cosk_ref[...], sink_ref[...]).astype(BF16)
    vb = v_ref[...].astype(BF16)
    ckb = ck_ref[...].astype(BF16)
    cvb = cv_ref[...].astype(BF16)
    scale = HEAD_DIM ** -0.5
    nt = (((1,), (1,)), ((), ()))
    for h in range(N_HEADS // N_KV_HEADS):
        sl = slice(h * HEAD_DIM, (h + 1) * HEAD_DIM)
        qh = _rope(_rms_head(q_ref[:, sl], qg_ref[...]), cosq_ref[...], sinq_ref[...]).astype(BF16)
        s_c = lax.dot_general(qh, ckb, nt, preferred_element_type=F32) * scale
        s_l = lax.dot_general(qh, kn, nt, preferred_element_type=F32) * scale
        m = jnp.maximum(jnp.max(s_c, axis=-1, keepdims=True), jnp.max(s_l, axis=-1, keepdims=True))
        p_c = jnp.exp(s_c - m)
        p_l = jnp.exp(s_l - m)
        l = jnp.sum(p_c, axis=-1, keepdims=True) + jnp.sum(p_l, axis=-1, keepdims=True)
        o = (jnp.dot(p_c.astype(BF16), cvb, preferred_element_type=F32)
             + jnp.dot(p_l.astype(BF16), vb, preferred_element_type=F32)) / l
        o_ref[:, sl] = o.astype(o_ref.dtype)


def attention_latent(qkv, cache_k, cache_v, cos_f, sin_f, q_gain, k_gain, o_ctx):
    g = N_HEADS // N_KV_HEADS
    hd = HEAD_DIM
    tq = min(ATTN_TQ, DEC_SEQ)
    nq = DEC_SEQ // tq
    q_row0 = _n_ctx() // tq
    k_row0 = _n_ctx() // DEC_SEQ
    q_map = lambda b, kh, qi: (q_row0 + b * nq + qi, kh)
    return pl.pallas_call(
        _attn_lat_body,
        grid=(DEC_BATCH, N_KV_HEADS, nq),
        in_specs=[pl.BlockSpec((tq, g * hd), q_map),
                  pl.BlockSpec((DEC_SEQ, hd), lambda b, kh, qi: (k_row0 + b, N_HEADS + kh)),
                  pl.BlockSpec((DEC_SEQ, hd), lambda b, kh, qi: (k_row0 + b, N_HEADS + N_KV_HEADS + kh)),
                  pl.BlockSpec((None, None, PAST_LEN, hd), lambda b, kh, qi: (b, kh, 0, 0)),
                  pl.BlockSpec((None, None, PAST_LEN, hd), lambda b, kh, qi: (b, kh, 0, 0)),
                  pl.BlockSpec((tq, hd), lambda b, kh, qi: (qi, 0)),
                  pl.BlockSpec((tq, hd), lambda b, kh, qi: (qi, 0)),
                  pl.BlockSpec((DEC_SEQ, hd), lambda b, kh, qi: (0, 0)),
                  pl.BlockSpec((DEC_SEQ, hd), lambda b, kh, qi: (0, 0)),
                  pl.BlockSpec((1, hd), lambda b, kh, qi: (0, 0)),
                  pl.BlockSpec((1, hd), lambda b, kh, qi: (0, 0)),
                  pl.BlockSpec(memory_space=pl.ANY)],
        out_specs=pl.BlockSpec((tq, g * hd), q_map),
        out_shape=jax.ShapeDtypeStruct(o_ctx.shape, o_ctx.dtype),
        input_output_aliases={11: 0},
        compiler_params=_params("arbitrary", "arbitrary", "arbitrary"),
    )(qkv, qkv, qkv, cache_k, cache_v, cos_f, sin_f, cos_f, sin_f,
      q_gain.reshape(1, hd), k_gain.reshape(1, hd), o_ctx)


def rope_tables():
    rows = DEC_SEQ // GRID_W
    r_idx, c_idx = jnp.meshgrid(jnp.arange(rows), jnp.arange(GRID_W), indexing="ij")
    r_idx = r_idx.reshape(-1).astype(F32)
    c_idx = c_idx.reshape(-1).astype(F32)
    n_freq = HEAD_DIM // 4
    inv = ROPE_THETA ** (-jnp.arange(n_freq, dtype=F32) / n_freq)
    ang = jnp.concatenate([r_idx[:, None] * inv, c_idx[:, None] * inv], axis=-1)
    cos, sin = jnp.cos(ang), jnp.sin(ang)
    return jnp.concatenate([cos, cos], axis=-1), jnp.concatenate([-sin, sin], axis=-1)


def _mix_body(x_ref, xp_ref, xn_ref, gain_ref, sh_ref, sc_ref, mu_ref, *o_refs):
    i = pl.program_id(0)
    tm = x_ref.shape[0]
    n_ctx_blocks = _n_ctx() // tm
    per_seq = DEC_SEQ // tm
    j = (i - n_ctx_blocks) % per_seq
    is_first = jnp.logical_or(i < n_ctx_blocks, j == 0)
    is_last = jnp.logical_or(i < n_ctx_blocks, j == per_seq - 1)
    gain, sh, sc = gain_ref[...], sh_ref[...], sc_ref[...]
    h = _modulated(x_ref[...], gain, sh, sc)
    h_prev = jnp.where(is_first, 0.0, _modulated(xp_ref[7:8, :], gain, sh, sc))
    h_next = jnp.where(is_last, 0.0, _modulated(xn_ref[0:1, :], gain, sh, sc))
    row = lax.broadcasted_iota(jnp.int32, h.shape, 0)
    prev = jnp.where(row == 0, h_prev, pltpu.roll(h, 1, 0))
    nxt = jnp.where(row == tm - 1, h_next, pltpu.roll(h, tm - 1, 0))
    xx = 0.5 * (prev + nxt) - h
    for n, o_ref in enumerate(o_refs):
        o_ref[...] = (h + xx * mu_ref[n:n + 1, :]).astype(o_ref.dtype)


def rwkv_mix(x, gain, shift, scale, mu):
    n, d = x.shape
    tm = ROW_BLOCK
    rb = tm // 8
    last8 = n // 8 - 1
    return pl.pallas_call(
        _mix_body,
        grid=(n // tm,),
        in_specs=[pl.BlockSpec((tm, d), lambda i: (i, 0)),
                  pl.BlockSpec((8, d), lambda i: (jnp.maximum(i * rb - 1, 0), 0)),
                  pl.BlockSpec((8, d), lambda i: (jnp.minimum((i + 1) * rb, last8), 0)),
                  pl.BlockSpec((1, d), lambda i: (0, 0)),
                  _group_spec(d, tm), _group_spec(d, tm),
                  pl.BlockSpec((mu.shape[0], d), lambda i: (0, 0))],
        out_specs=[pl.BlockSpec((tm, d), lambda i: (i, 0)) for _ in range(6)],
        out_shape=[jax.ShapeDtypeStruct((n, d), BF16) for _ in range(6)],
        compiler_params=_params("arbitrary"),
    )(x, x, x, gain.reshape(1, d), shift, scale, mu)


def _seg_sum(x, ones_bd):
    hi, lo = _split_bf16(x)
    return (jnp.dot(hi, ones_bd, preferred_element_type=F32)
            + jnp.dot(lo, ones_bd, preferred_element_type=F32))


def _scan_body(r_ref, k_ref, v_ref, wl_ref, al_ref, g_ref, kk_ref, ka_ref, rk_ref, lnw_ref, lnb_ref,
               s0_ref, *rest, reverse, add_prev):
    if add_prev:
        prev_ref, o_ref, sfin_ref, s_scr = rest
    else:
        o_ref, sfin_ref, s_scr = rest
    L = SCAN_CHUNK
    n_pairs = s_scr.shape[0]
    step = pl.program_id(0)
    n_chunks = pl.num_programs(0)
    c = (n_chunks - 1 - step) if reverse else step
    ctx_chunks = _n_ctx() // L
    per_ctx, per_lat = SEQ // L, DEC_SEQ // L
    pos = jnp.where(c < ctx_chunks, c % per_ctx, (c - ctx_chunks) % per_lat)
    seq_len = jnp.where(c < ctx_chunks, per_ctx, per_lat)
    first_pos = seq_len - 1 if reverse else 0
    last_pos = 0 if reverse else seq_len - 1

    @pl.when(pos == first_pos)
    def _():
        s_scr[...] = s0_ref[...]

    ri = lax.broadcasted_iota(jnp.int32, (L, L), 0)
    ci = lax.broadcasted_iota(jnp.int32, (L, L), 1)
    before = (ci > ri) if reverse else (ci < ri)
    incl_bf = jnp.logical_or(before, ci == ri).astype(BF16)
    eye = (ci == ri).astype(F32)
    r2 = lax.broadcasted_iota(jnp.int32, (2 * L, 2 * L), 0)
    c2 = lax.broadcasted_iota(jnp.int32, (2 * L, 2 * L), 1)
    t2, j2 = r2 % L, c2 % L
    before2 = (j2 > t2) if reverse else (j2 < t2)
    mask2 = jnp.logical_or(before2, jnp.logical_and(r2 >= L, j2 == t2))
    hw = RWKV_HEAD
    lr = lax.broadcasted_iota(jnp.int32, (LANES, LANES), 0)
    lc = lax.broadcasted_iota(jnp.int32, (LANES, LANES), 1)
    bd = (lr // hw) == (lc // hw)
    ones_bd = bd.astype(BF16)
    lane = lax.broadcasted_iota(jnp.int32, (L, LANES), 1)
    head_a = lane < hw
    lane2 = lax.broadcasted_iota(jnp.int32, (2 * L, LANES), 1)
    head_a2 = lane2 < hw
    nt = (((1,), (1,)), ((), ()))
    inv_hw = 1.0 / hw

    def pair(p, carry):
        sl = pl.ds(pl.multiple_of(p * LANES, LANES), LANES)
        r, k, v = r_ref[:, sl], k_ref[:, sl], v_ref[:, sl]
        lw = DECAY_SCALE * jax.nn.sigmoid(wl_ref[:, sl])
        ag = jax.nn.sigmoid(al_ref[:, sl])
        kkf = k * kk_ref[:, sl]
        kk = kkf / jnp.maximum(jnp.sqrt(_seg_sum(kkf * kkf, ones_bd)), 1e-12)
        kd = k * (1.0 + (ag - 1.0) * ka_ref[:, sl])
        a = -kk
        b = kk * ag
        w_hi = lw.astype(BF16)
        w_r1 = lw - w_hi.astype(F32)
        w_mid = w_r1.astype(BF16)
        w_lo = (w_r1 - w_mid.astype(F32)).astype(BF16)
        cs = (jnp.dot(incl_bf, w_hi, preferred_element_type=F32)
              + jnp.dot(incl_bf, w_mid, preferred_element_type=F32)
              + jnp.dot(incl_bf, w_lo, preferred_element_type=F32))
        e_neg = jnp.exp(-cs)
        a_t = jnp.exp(cs - lw) * a
        r_t = jnp.exp(cs) * r
        b_t = e_neg * b
        k_t = e_neg * kd
        c_tot = cs[0:1, :] if reverse else cs[L - 1:L, :]
        ar = jnp.concatenate([a_t, r_t], axis=0)
        bk = jnp.concatenate([b_t, k_t], axis=0).astype(BF16)
        ar_a = jnp.where(head_a2, ar, 0.0).astype(BF16)
        ar_b = jnp.where(head_a2, 0.0, ar).astype(BF16)
        s_vk = s_scr[p]
        a_s = lax.dot_general(ar.astype(BF16), s_vk.astype(BF16), nt, preferred_element_type=F32)
        zv = jnp.concatenate([jnp.zeros_like(v), v], axis=0).astype(BF16)
        tops, bots = [], []
        for ar_h in (ar_a, ar_b):
            gm = lax.dot_general(ar_h, bk, nt, preferred_element_type=F32)
            tm = jnp.where(mask2, gm, 0.0)
            tops.append(tm[:L])
            bots.append(tm[L:])
        takv = [jnp.dot(t.astype(BF16), zv, preferred_element_type=F32) for t in tops]
        w0 = (a_s[:L] + jnp.where(head_a, takv[0], takv[1])).astype(BF16)
        us = []
        for t in tops:
            tab = t[:, :L]
            m = eye + tab
            pw = tab
            n = 1
            while n < L // 2:
                pwb = pw.astype(BF16)
                pw = jnp.dot(pwb, pwb, preferred_element_type=F32)
                m = m + jnp.dot(m.astype(BF16), pw.astype(BF16), preferred_element_type=F32)
                n *= 2
            us.append(jnp.dot(m.astype(BF16), w0, preferred_element_type=F32))
        u = jnp.where(head_a, us[0], us[1])
        uv = jnp.concatenate([u, v], axis=0)
        uvb = uv.astype(BF16)
        ys = [jnp.dot(t.astype(BF16), uvb, preferred_element_type=F32) for t in bots]
        y = a_s[L:] + jnp.where(head_a, ys[0], ys[1])
        upd = jnp.dot(uv.T.astype(BF16), bk, preferred_element_type=F32)
        s_scr[p] = jnp.where(bd, s_vk + upd, 0.0) * jnp.exp(c_tot)
        mean = _seg_sum(y, ones_bd) * inv_hw
        dy = y - mean
        var = _seg_sum(dy * dy, ones_bd) * inv_hw
        yn = dy * lax.rsqrt(var + GN_EPS) * lnw_ref[:, sl] + lnb_ref[:, sl]
        bonus = _seg_sum(r * kd * rk_ref[:, sl], ones_bd) * v
        out = (yn + bonus) * g_ref[:, sl]
        if add_prev:
            out = out + prev_ref[:, sl].astype(F32)
        o_ref[:, sl] = out.astype(o_ref.dtype)
        return carry

    lax.fori_loop(0, n_pairs, pair, 0)

    @pl.when(pos == last_pos)
    def _():
        sfin_ref[...] = s_scr[...]


def wkv_scan_dir(r, k, v, wl, al, g, k_k, k_a, r_k, lnx_w, lnx_b, s0, reverse, prev=None):
    n, d = r.shape
    L = SCAN_CHUNK
    n_chunks = n // L
    n_pairs = d // LANES
    ctx_chunks = _n_ctx() // L

    def cidx(s):
        return (n_chunks - 1 - s) if reverse else s

    def seq_of(s):
        c = cidx(s)
        return jnp.where(c < ctx_chunks, c // (SEQ // L), BATCH + (c - ctx_chunks) // (DEC_SEQ // L))

    tok = pl.BlockSpec((L, d), lambda s: (cidx(s), 0))
    vec = pl.BlockSpec((1, d), lambda s: (0, 0))
    st = pl.BlockSpec((None, n_pairs, LANES, LANES), lambda s: (seq_of(s), 0, 0, 0))
    add_prev = prev is not None
    body = functools.partial(_scan_body, reverse=reverse, add_prev=add_prev)
    args = [r, k, v, wl, al, g, k_k.reshape(1, d), k_a.reshape(1, d), r_k.reshape(1, d),
            lnx_w.reshape(1, d), lnx_b.reshape(1, d), s0]
    in_specs = [tok] * 6 + [vec] * 5 + [st]
    if add_prev:
        args.append(prev)
        in_specs.append(tok)
    return pl.pallas_call(
        body,
        grid=(n_chunks,),
        in_specs=in_specs,
        out_specs=[tok, st],
        out_shape=[jax.ShapeDtypeStruct((n, d), BF16 if add_prev else F32),
                   jax.ShapeDtypeStruct(s0.shape, F32)],
        scratch_shapes=[pltpu.VMEM((n_pairs, LANES, LANES), F32)],
        compiler_params=_params("arbitrary"),
    )(*args)


def _pair_states(s):
    n_seq, h, hv, hk = s.shape
    s = s.reshape(n_seq, h // 2, 2, hv, hk)
    z = jnp.zeros_like(s[:, :, 0])
    top = jnp.concatenate([s[:, :, 0], z], axis=-1)
    bot = jnp.concatenate([z, s[:, :, 1]], axis=-1)
    return jnp.concatenate([top, bot], axis=-2)


def _unpair_states(s):
    n_seq, n_pairs = s.shape[:2]
    hw = RWKV_HEAD
    a = s[:, :, :hw, :hw]
    b = s[:, :, hw:, hw:]
    return jnp.stack([a, b], axis=2).reshape(n_seq, 2 * n_pairs, hw, hw)


def _expert_up_body(be_ref, first_ref, used_ref, x_ref, wg_ref, wu_ref, o_ref, wg_bf, wu_bf):
    del be_ref
    b = pl.program_id(1)

    @pl.when(b < used_ref[0])
    def _():
        @pl.when(first_ref[b] == 1)
        def _():
            wg_bf[...] = wg_ref[...].astype(BF16)
            wu_bf[...] = wu_ref[...].astype(BF16)

        x = x_ref[...]
        gate = jnp.dot(x, wg_bf[...], preferred_element_type=F32)
        up = jnp.dot(x, wu_bf[...], preferred_element_type=F32)
        o_ref[...] = (_silu(gate) * up).astype(o_ref.dtype)


def expert_up(xs, w_gate, w_up, block_e, first, used, tf=512):
    n_slots, d = xs.shape
    ff = w_gate.shape[-1]
    bm = ROW_BLOCK
    tf = min(tf, ff)
    n_blocks = n_slots // bm

    def row(f, b, be, fi, us):
        return jnp.minimum(b, us[0] - 1)

    return pl.pallas_call(
        _expert_up_body,
        grid_spec=pltpu.PrefetchScalarGridSpec(
            num_scalar_prefetch=3,
            grid=(ff // tf, n_blocks),
            in_specs=[pl.BlockSpec((bm, d), lambda f, b, be, fi, us: (row(f, b, be, fi, us), 0)),
                      pl.BlockSpec((None, d, tf), lambda f, b, be, fi, us: (be[row(f, b, be, fi, us)], 0, f)),
                      pl.BlockSpec((None, d, tf), lambda f, b, be, fi, us: (be[row(f, b, be, fi, us)], 0, f))],
            out_specs=pl.BlockSpec((bm, tf), lambda f, b, be, fi, us: (row(f, b, be, fi, us), f)),
            scratch_shapes=[pltpu.VMEM((d, tf), BF16), pltpu.VMEM((d, tf), BF16)]),
        out_shape=jax.ShapeDtypeStruct((n_slots, ff), BF16),
        compiler_params=_params("arbitrary", "arbitrary"),
    )(block_e, first, used, xs, w_gate, w_up)


def _expert_down_body(be_ref, first_ref, used_ref, h_ref, wd_ref, o_ref, wd_bf):
    del be_ref
    b = pl.program_id(1)

    @pl.when(b < used_ref[0])
    def _():
        @pl.when(first_ref[b] == 1)
        def _():
            wd_bf[...] = wd_ref[...].astype(BF16)

        o_ref[...] = jnp.dot(h_ref[...], wd_bf[...], preferred_element_type=F32).astype(o_ref.dtype)


def expert_down(hm, w_down, block_e, first, used, tn=1024):
    n_slots, ff = hm.shape
    d = w_down.shape[-1]
    bm = ROW_BLOCK
    tn = min(tn, d)
    n_blocks = n_slots // bm

    def row(j, b, be, fi, us):
        return jnp.minimum(b, us[0] - 1)

    return pl.pallas_call(
        _expert_down_body,
        grid_spec=pltpu.PrefetchScalarGridSpec(
            num_scalar_prefetch=3,
            grid=(d // tn, n_blocks),
            in_specs=[pl.BlockSpec((bm, ff), lambda j, b, be, fi, us: (row(j, b, be, fi, us), 0)),
                      pl.BlockSpec((None, ff, tn), lambda j, b, be, fi, us: (be[row(j, b, be, fi, us)], 0, j))],
            out_specs=pl.BlockSpec((bm, tn), lambda j, b, be, fi, us: (row(j, b, be, fi, us), j)),
            scratch_shapes=[pltpu.VMEM((ff, tn), BF16)]),
        out_shape=jax.ShapeDtypeStruct((n_slots, d), F32),
        compiler_params=_params("arbitrary", "arbitrary"),
    )(block_e, first, used, hm, w_down)


def route(scores, b_router):
    n = scores.shape[0]
    bm = ROW_BLOCK
    _, idx = lax.top_k(scores + b_router.astype(F32), TOP_K)
    sel = jnp.take_along_axis(scores, idx, axis=-1)
    gw = sel / jnp.sum(sel, axis=-1, keepdims=True) * ROUTED_SCALE
    a = n * TOP_K
    n_blocks = a // bm + N_EXPERTS
    e_flat = idx.reshape(-1)
    onehot = (e_flat[:, None] == jnp.arange(N_EXPERTS)[None, :]).astype(jnp.int32)
    rank = jnp.take_along_axis(jnp.cumsum(onehot, axis=0), e_flat[:, None], axis=1)[:, 0] - 1
    counts = jnp.sum(onehot, axis=0)
    padded = (counts + bm - 1) // bm * bm
    pad_end = jnp.cumsum(padded)
    pad_start = pad_end - padded
    dest = (pad_start[e_flat] + rank).astype(jnp.int32)
    tok_flat = jnp.arange(a, dtype=jnp.int32) // TOP_K
    slot_tok = jnp.full((n_blocks * bm,), n, jnp.int32).at[dest].set(tok_flat)
    block_e = jnp.minimum(jnp.searchsorted(pad_end, jnp.arange(n_blocks) * bm, side="right"),
                          N_EXPERTS - 1).astype(jnp.int32)
    first = jnp.concatenate([jnp.ones((1,), jnp.int32),
                             (block_e[1:] != block_e[:-1]).astype(jnp.int32)])
    used = (pad_end[-1:] // bm).astype(jnp.int32)
    return gw, dest.reshape(n, TOP_K), slot_tok, block_e, first, used


def kernel(x_prompt, x_sample, c, cache_k, cache_v, state_wkv, c_ctx, w_ada, b_ada, norm_mix, norm_ffn, attn_w_qkv, attn_w_o, attn_q_gain, attn_k_gain, rw_mu, rw_w_r, rw_w_k, rw_w_v, rw_w_o, rw_w0, rw_w1, rw_w2, rw_a0, rw_a1, rw_a2, rw_g1, rw_g2, rw_k_k, rw_k_a, rw_r_k, rw_lnx_w, rw_lnx_b, moe_w_router, moe_b_router, moe_w_gate, moe_w_up, moe_w_down, sh_w_gate, sh_w_up, sh_w_down):
    d = D_MODEL
    n_ctx, n = _n_ctx(), _n_tok()
    n_grp = 1 + DEC_BATCH
    x = jnp.concatenate([x_prompt.reshape(n_ctx, d), x_sample.reshape(n - n_ctx, d)], axis=0)

    cvec8 = jnp.zeros((8, d), F32).at[0].set(c_ctx).at[1:n_grp].set(c)
    mods = ada_all(cvec8, w_ada, b_ada)

    def mod(layer, m):
        return mods[layer, :n_grp, m * d:(m + 1) * d].reshape(n_grp, 1, d)

    tm, tn = min(MM_TM, n), min(MM_TN, d)
    resid_gate = lambda acc, res, gate: res + gate * acc

    new_k = new_v = new_s = None
    for i in range(DEPTH):
        j = i // 2
        sh1, sc1, g1, sh2, sc2, g2 = [mod(i, m) for m in range(N_MOD)]
        if i % 2 == 0:
            h = modulate(x, norm_mix[i], sh1, sc1)
            qkv = matmul(h, [attn_w_qkv[j]])
            o_ctx, kc, vc = attention_context(qkv, attn_q_gain[j], attn_k_gain[j])
            cos_f, sin_f = rope_tables()
            o = attention_latent(qkv, jnp.swapaxes(cache_k[:, j], 1, 2), jnp.swapaxes(cache_v[:, j], 1, 2),
                                 cos_f, sin_f, attn_q_gain[j], attn_k_gain[j], o_ctx)
            new_k = kc.reshape(BATCH, 1, SEQ, N_KV_HEADS, HEAD_DIM)
            new_v = vc.reshape(BATCH, 1, SEQ, N_KV_HEADS, HEAD_DIM)
            w_o = attn_w_o[j]
        else:
            xr, xw, xk, xv, xa, xg = rwkv_mix(x, norm_mix[i], sh1, sc1, rw_mu[j])
            r = matmul(xr, [rw_w_r[j]])
            k = matmul(xk, [rw_w_k[j]])
            v = matmul(xv, [rw_w_v[j]])
            lw = DECAY_LORA
            la = AAA_LORA
            lg = GATE_LORA
            lgp = -(-lg // LANES) * LANES
            w_mid = matmul(xw, [jnp.concatenate([rw_w1[j, 0], rw_w1[j, 1]], axis=1)],
                           epilogue=jnp.tanh, out_dtype=BF16)
            a_mid = matmul(xa, [jnp.concatenate([rw_a1[j, 0], rw_a1[j, 1]], axis=1)], out_dtype=BF16)
            g1p = jnp.pad(rw_g1[j], ((0, 0), (0, 0), (0, lgp - lg)))
            g2p = jnp.pad(rw_g2[j], ((0, 0), (0, lgp - lg), (0, 0)))
            g_mid = matmul(xg, [jnp.concatenate([g1p[0], g1p[1]], axis=1)],
                           epilogue=jax.nn.sigmoid, out_dtype=BF16)
            add_vec = lambda acc, vec: acc + vec
            n_seq = BATCH + DEC_BATCH
            s0_all = jnp.concatenate(
                [jnp.zeros((BATCH, 2) + state_wkv.shape[3:], F32), state_wkv[:, j]], axis=0)
            o = None
            finals = []
            for dr in range(2):
                wl = matmul(w_mid[:, dr * lw:(dr + 1) * lw], [rw_w2[j, dr]], epilogue=add_vec,
                            extra=[(rw_w0[j, dr].reshape(1, d), _row_vec_spec(tn))])
                al = matmul(a_mid[:, dr * la:(dr + 1) * la], [rw_a2[j, dr]], epilogue=add_vec,
                            extra=[(rw_a0[j, dr].reshape(1, d), _row_vec_spec(tn))])
                gg = matmul(g_mid[:, dr * lgp:(dr + 1) * lgp], [g2p[dr]])
                o, s_fin = wkv_scan_dir(r, k, v, wl, al, gg, rw_k_k[j], rw_k_a[j], rw_r_k[j, dr],
                                        rw_lnx_w[j], rw_lnx_b[j], _pair_states(s0_all[:, dr]),
                                        reverse=(dr == 1), prev=o)
                finals.append(_unpair_states(s_fin[:BATCH]))
            if not o.dtype == BF16:
                o = o.astype(BF16)
            new_s = jnp.stack(finals, axis=1)[:, None]
            w_o = rw_w_o[j]
        x = matmul(o, [w_o], epilogue=resid_gate,
                   extra=[(x, _tile_spec(tm, tn)), (g1, _gate_spec(tm, tn))])

        h2, scores = modulate_and_route(x, norm_ffn[i], sh2, sc2, moe_w_router[i])
        gw, dest, slot_tok, block_e, first, used = route(scores, moe_b_router[i])
        h2_pad = jnp.concatenate([h2, jnp.zeros((1, d), BF16)], axis=0)
        xs = jnp.take(h2_pad, slot_tok, axis=0)
        hm = expert_up(xs, moe_w_gate[i], moe_w_up[i], block_e, first, used)
        yb = expert_down(hm, moe_w_down[i], block_e, first, used)
        routed = jnp.sum(jnp.take(yb, dest, axis=0) * gw[:, :, None], axis=1)
        hs = matmul(h2, [sh_w_gate[i], sh_w_up[i]], epilogue=lambda a, b: _silu(a) * b, out_dtype=BF16)
        moe_out = lambda acc, rt, res, gate: res + gate * (acc + rt)
        x = matmul(hs, [sh_w_down[i]], epilogue=moe_out,
                   extra=[(routed, _tile_spec(tm, tn)), (x, _tile_spec(tm, tn)), (g2, _gate_spec(tm, tn))])

    y_prompt = x[:n_ctx].reshape(BATCH, SEQ, d)
    y_sample = x[n_ctx:].reshape(DEC_BATCH, DEC_SEQ, d)
    return (y_prompt, y_sample, new_k, new_v, new_s)
```

```python
import functools

import jax
import jax.numpy as jnp
from jax import lax
from jax.experimental import pallas as pl
from jax.experimental.pallas import tpu as pltpu

D_MODEL = 4096
BATCH = 16
SEQ = 256
DEPTH = 2
DEC_BATCH = 2
DEC_SEQ = 1024
PAST_LEN = 512
GRID_W = 64
N_HEADS = 32
N_KV_HEADS = 8
HEAD_DIM = 128
ROPE_THETA = 10000.0
RWKV_HEAD = 64
DECAY_LORA = 128
AAA_LORA = 128
GATE_LORA = 480
N_EXPERTS = 64
TOP_K = 8
EXPERT_FF = 1024
SHARED_FF = 1024
ROUTED_SCALE = 2.5
N_MOD = 6
NORM_EPS = 1e-6
GN_EPS = 64e-5
DECAY_SCALE = -0.606531

F32 = jnp.float32
BF16 = jnp.bfloat16

V7X_VMEM_LIMIT_BYTES = 56 * 1024 * 1024
LANES = 128
ROW_BLOCK = 256
MM_TM = 512
MM_TN = 512
SCAN_CHUNK = 64
SCAN_INTERLEAVE = 4
ATTN_TQ = 512


def _params(*sem):
    return pltpu.CompilerParams(dimension_semantics=sem, vmem_limit_bytes=V7X_VMEM_LIMIT_BYTES)


def _n_ctx():
    return BATCH * SEQ


def _n_tok():
    return BATCH * SEQ + DEC_BATCH * DEC_SEQ


def _group_of_block(i, tm):
    n_ctx_blocks = _n_ctx() // tm
    return jnp.where(i < n_ctx_blocks, 0, 1 + (i - n_ctx_blocks) // (DEC_SEQ // tm))


def _silu(x):
    return x * jax.nn.sigmoid(x)


def _ada_body(c_ref, w_ref, b_ref, o_ref):
    s = _silu(c_ref[...]).astype(BF16)
    o_ref[...] = jnp.dot(s, w_ref[...].astype(BF16), preferred_element_type=F32) + b_ref[...]


def ada_all(cvec8, w_ada, b_ada, tn=512):
    depth, d, n6 = w_ada.shape
    return pl.pallas_call(
        _ada_body,
        grid=(depth, n6 // tn),
        in_specs=[pl.BlockSpec((8, d), lambda l, j: (0, 0)),
                  pl.BlockSpec((None, d, tn), lambda l, j: (l, 0, j)),
                  pl.BlockSpec((None, 1, tn), lambda l, j: (l, 0, j))],
        out_specs=pl.BlockSpec((None, 8, tn), lambda l, j: (l, 0, j)),
        out_shape=jax.ShapeDtypeStruct((depth, 8, n6), F32),
        compiler_params=_params("arbitrary", "arbitrary"),
        name="ada",
    )(cvec8, w_ada, b_ada.reshape(depth, 1, n6))


def _modulated(x, gain, shift, scale):
    y = x * lax.rsqrt(jnp.mean(x * x, axis=-1, keepdims=True) + NORM_EPS) * gain
    return y * (1.0 + scale) + shift


def _modulate_body(x_ref, gain_ref, sh_ref, sc_ref, o_ref):
    o_ref[...] = _modulated(x_ref[...], gain_ref[...], sh_ref[...], sc_ref[...]).astype(o_ref.dtype)


def _group_spec(d, tm):
    return pl.BlockSpec((None, 1, d), lambda i: (_group_of_block(i, tm), 0, 0))


def modulate(x, gain, shift, scale):
    n, d = x.shape
    tm = ROW_BLOCK
    return pl.pallas_call(
        _modulate_body,
        grid=(n // tm,),
        in_specs=[pl.BlockSpec((tm, d), lambda i: (i, 0)),
                  pl.BlockSpec((1, d), lambda i: (0, 0)),
                  _group_spec(d, tm), _group_spec(d, tm)],
        out_specs=pl.BlockSpec((tm, d), lambda i: (i, 0)),
        out_shape=jax.ShapeDtypeStruct((n, d), BF16),
        compiler_params=_params("arbitrary"),
        name="modulate",
    )(x, gain.reshape(1, d), shift, scale)


def _split_bf16(x):
    hi = x.astype(BF16)
    lo = (x - hi.astype(F32)).astype(BF16)
    return hi, lo


def _bf16_bits(x):
    return lax.bitcast_convert_type(x.astype(BF16).astype(F32), jnp.int32)


_UPPER_HALF = -65536


def _packed_row_words(d):
    return d // 2 + d // 4


def _router_body(x_ref, gain_ref, sh_ref, sc_ref, wr_ref, o_ref, p_ref, s_ref):
    h = _modulated(x_ref[...], gain_ref[...], sh_ref[...], sc_ref[...])
    o_ref[...] = h.astype(BF16)
    half = h.shape[1] // 2
    p_ref[:, :half] = (lax.shift_right_logical(_bf16_bits(h[:, :half]), 16)
                       | (_bf16_bits(h[:, half:]) & _UPPER_HALF))
    p_ref[:, half:] = jnp.zeros((h.shape[0], p_ref.shape[1] - half), jnp.int32)
    h_hi, h_lo = _split_bf16(h)
    w_hi, w_lo = _split_bf16(wr_ref[...])
    logits = (jnp.dot(h_hi, w_hi, preferred_element_type=F32)
              + jnp.dot(h_lo, w_hi, preferred_element_type=F32)
              + jnp.dot(h_hi, w_lo, preferred_element_type=F32))
    s_ref[...] = jax.nn.sigmoid(logits)


def modulate_and_route(x, gain, shift, scale, w_router):
    n, d = x.shape
    e = w_router.shape[1]
    tm = ROW_BLOCK
    return pl.pallas_call(
        _router_body,
        grid=(n // tm,),
        in_specs=[pl.BlockSpec((tm, d), lambda i: (i, 0)),
                  pl.BlockSpec((1, d), lambda i: (0, 0)),
                  _group_spec(d, tm), _group_spec(d, tm),
                  pl.BlockSpec((d, e), lambda i: (0, 0))],
        out_specs=[pl.BlockSpec((tm, d), lambda i: (i, 0)),
                   pl.BlockSpec((tm, _packed_row_words(d)), lambda i: (i, 0)),
                   pl.BlockSpec((tm, e), lambda i: (i, 0))],
        out_shape=[jax.ShapeDtypeStruct((n, d), BF16), jax.ShapeDtypeStruct((n, _packed_row_words(d)), jnp.int32),
                   jax.ShapeDtypeStruct((n, e), F32)],
        compiler_params=_params("arbitrary"),
        name="modulate_route",
    )(x, gain.reshape(1, d), shift, scale, w_router)


def _mm_body(*refs, n_w, n_extra, epilogue):
    a_ref = refs[0]
    w_refs = refs[1:1 + n_w]
    extra = refs[1 + n_w:1 + n_w + n_extra]
    o_ref = refs[1 + n_w + n_extra]
    w_bf = refs[2 + n_w + n_extra:]

    @pl.when(pl.program_id(1) == 0)
    def _():
        for w_ref, w_s in zip(w_refs, w_bf):
            w_s[...] = w_ref[...].astype(BF16)

    a = a_ref[...]
    accs = [jnp.dot(a, w_s[...], preferred_element_type=F32) for w_s in w_bf]
    o_ref[...] = epilogue(*accs, *[r[...] for r in extra]).astype(o_ref.dtype)


def _stacked(w, *lead):
    flat = 0
    for size, idx in zip(w.shape[:len(lead)], lead):
        flat = flat * size + idx
    return w.reshape((-1,) + w.shape[len(lead):]), flat


def matmul(name, a, ws, epilogue=None, extra=(), out_dtype=F32, tm=None, tn=None):
    m, k = a.shape
    n = ws[0][0].shape[2]
    tm = min(tm or MM_TM, m)
    tn = min(tn or MM_TN, n)
    if epilogue is None:
        epilogue = lambda acc: acc
    body = functools.partial(_mm_body, n_w=len(ws), n_extra=len(extra), epilogue=epilogue)

    def w_spec(idx):
        return pl.BlockSpec((None, k, tn), lambda j, i: (idx, 0, j))

    return pl.pallas_call(
        body,
        grid=(n // tn, m // tm),
        in_specs=([pl.BlockSpec((tm, k), lambda j, i: (i, 0))]
                  + [w_spec(idx) for _, idx in ws]
                  + [spec for _, spec in extra]),
        out_specs=pl.BlockSpec((tm, tn), lambda j, i: (i, j)),
        out_shape=jax.ShapeDtypeStruct((m, n), out_dtype),
        scratch_shapes=[pltpu.VMEM((k, tn), BF16) for _ in ws],
        compiler_params=_params("arbitrary", "arbitrary"),
        name=name,
    )(a, *[w for w, _ in ws], *[arr for arr, _ in extra])


def _row_vec_spec(tn):
    return pl.BlockSpec((1, tn), lambda j, i: (0, j))


def _gate_spec(tm, tn):
    return pl.BlockSpec((None, 1, tn), lambda j, i: (_group_of_block(i, tm), 0, j))


def _tile_spec(tm, tn):
    return pl.BlockSpec((tm, tn), lambda j, i: (i, j))


def _rms_head(x, gain):
    return x * lax.rsqrt(jnp.mean(x * x, axis=-1, keepdims=True) + NORM_EPS) * gain


def _rope(x, cos_f, sin_f):
    return x * cos_f + pltpu.roll(x, HEAD_DIM // 2, 1) * sin_f


def _attn_ctx_body(q_ref, k_ref, v_ref, qg_ref, kg_ref, o_ref, kc_ref, vc_ref):
    kn = _rms_head(k_ref[...], kg_ref[...])
    kc_ref[...] = kn
    v = v_ref[...]
    vc_ref[...] = v
    kb = kn.astype(BF16)
    vb = v.astype(BF16)
    scale = HEAD_DIM ** -0.5
    for h in range(N_HEADS // N_KV_HEADS):
        sl = slice(h * HEAD_DIM, (h + 1) * HEAD_DIM)
        qh = _rms_head(q_ref[:, sl], qg_ref[...]).astype(BF16)
        s = lax.dot_general(qh, kb, (((1,), (1,)), ((), ())), preferred_element_type=F32) * scale
        p = jnp.exp(s - jnp.max(s, axis=-1, keepdims=True))
        l = jnp.sum(p, axis=-1, keepdims=True)
        o = jnp.dot(p.astype(BF16), vb, preferred_element_type=F32) / l
        o_ref[:, sl] = o.astype(o_ref.dtype)


def attention_context(qkv, q_gain, k_gain):
    n = qkv.shape[0]
    g = N_HEADS // N_KV_HEADS
    hd = HEAD_DIM
    return pl.pallas_call(
        _attn_ctx_body,
        grid=(BATCH, N_KV_HEADS),
        in_specs=[pl.BlockSpec((SEQ, g * hd), lambda b, kh: (b, kh)),
                  pl.BlockSpec((SEQ, hd), lambda b, kh: (b, N_HEADS + kh)),
                  pl.BlockSpec((SEQ, hd), lambda b, kh: (b, N_HEADS + N_KV_HEADS + kh)),
                  pl.BlockSpec((1, hd), lambda b, kh: (0, 0)),
                  pl.BlockSpec((1, hd), lambda b, kh: (0, 0))],
        out_specs=[pl.BlockSpec((SEQ, g * hd), lambda b, kh: (b, kh)),
                   pl.BlockSpec((SEQ, hd), lambda b, kh: (b, kh)),
                   pl.BlockSpec((SEQ, hd), lambda b, kh: (b, kh))],
        out_shape=[jax.ShapeDtypeStruct((n, N_HEADS * hd), BF16),
                   jax.ShapeDtypeStruct((_n_ctx(), N_KV_HEADS * hd), F32),
                   jax.ShapeDtypeStruct((_n_ctx(), N_KV_HEADS * hd), F32)],
        compiler_params=_params("arbitrary", "arbitrary"),
        name="attn_ctx",
    )(qkv, qkv, qkv, q_gain.reshape(1, hd), k_gain.reshape(1, hd))


def _attn_lat_body(q_ref, k_ref, v_ref, ck_ref, cv_ref, cosq_ref, sinq_ref, cosk_ref, sink_ref,
                   qg_ref, kg_ref, o_in_ref, o_ref):
    del o_in_ref
    kn = _rope(_rms_head(k_ref[...], kg_ref[...]), cosk_ref[...], sink_ref[...]).astype(BF16)
    vb = v_ref[...].astype(BF16)
    ckb = ck_ref[...].astype(BF16)
    cvb = cv_ref[...].astype(BF16)
    scale = HEAD_DIM ** -0.5
    nt = (((1,), (1,)), ((), ()))
    for h in range(N_HEADS // N_KV_HEADS):
        sl = slice(h * HEAD_DIM, (h + 1) * HEAD_DIM)
        qh = _rope(_rms_head(q_ref[:, sl], qg_ref[...]), cosq_ref[...], sinq_ref[...]).astype(BF16)
        s_c = lax.dot_general(qh, ckb, nt, preferred_element_type=F32) * scale
        s_l = lax.dot_general(qh, kn, nt, preferred_element_type=F32) * scale
        m = jnp.maximum(jnp.max(s_c, axis=-1, keepdims=True), jnp.max(s_l, axis=-1, keepdims=True))
        p_c = jnp.exp(s_c - m)
        p_l = jnp.exp(s_l - m)
        l = jnp.sum(p_c, axis=-1, keepdims=True) + jnp.sum(p_l, axis=-1, keepdims=True)
        o = (jnp.dot(p_c.astype(BF16), cvb, preferred_element_type=F32)
             + jnp.dot(p_l.astype(BF16), vb, preferred_element_type=F32)) / l
        o_ref[:, sl] = o.astype(o_ref.dtype)


def attention_latent(qkv, cache_k, cache_v, cos_f, sin_f, q_gain, k_gain, o_ctx):
    g = N_HEADS // N_KV_HEADS
    hd = HEAD_DIM
    tq = min(ATTN_TQ, DEC_SEQ)
    nq = DEC_SEQ // tq
    q_row0 = _n_ctx() // tq
    k_row0 = _n_ctx() // DEC_SEQ
    q_map = lambda b, kh, qi: (q_row0 + b * nq + qi, kh)
    return pl.pallas_call(
        _attn_lat_body,
        grid=(DEC_BATCH, N_KV_HEADS, nq),
        in_specs=[pl.BlockSpec((tq, g * hd), q_map),
                  pl.BlockSpec((DEC_SEQ, hd), lambda b, kh, qi: (k_row0 + b, N_HEADS + kh)),
                  pl.BlockSpec((DEC_SEQ, hd), lambda b, kh, qi: (k_row0 + b, N_HEADS + N_KV_HEADS + kh)),
                  pl.BlockSpec((None, None, PAST_LEN, hd), lambda b, kh, qi: (b, kh, 0, 0)),
                  pl.BlockSpec((None, None, PAST_LEN, hd), lambda b, kh, qi: (b, kh, 0, 0)),
                  pl.BlockSpec((tq, hd), lambda b, kh, qi: (qi, 0)),
                  pl.BlockSpec((tq, hd), lambda b, kh, qi: (qi, 0)),
                  pl.BlockSpec((DEC_SEQ, hd), lambda b, kh, qi: (0, 0)),
                  pl.BlockSpec((DEC_SEQ, hd), lambda b, kh, qi: (0, 0)),
                  pl.BlockSpec((1, hd), lambda b, kh, qi: (0, 0)),
                  pl.BlockSpec((1, hd), lambda b, kh, qi: (0, 0)),
                  pl.BlockSpec(memory_space=pl.ANY)],
        out_specs=pl.BlockSpec((tq, g * hd), q_map),
        out_shape=jax.ShapeDtypeStruct(o_ctx.shape, o_ctx.dtype),
        input_output_aliases={11: 0},
        compiler_params=_params("arbitrary", "arbitrary", "arbitrary"),
        name="attn_lat",
    )(qkv, qkv, qkv, cache_k, cache_v, cos_f, sin_f, cos_f, sin_f,
      q_gain.reshape(1, hd), k_gain.reshape(1, hd), o_ctx)


def rope_tables():
    rows = DEC_SEQ // GRID_W
    r_idx, c_idx = jnp.meshgrid(jnp.arange(rows), jnp.arange(GRID_W), indexing="ij")
    r_idx = r_idx.reshape(-1).astype(F32)
    c_idx = c_idx.reshape(-1).astype(F32)
    n_freq = HEAD_DIM // 4
    inv = ROPE_THETA ** (-jnp.arange(n_freq, dtype=F32) / n_freq)
    ang = jnp.concatenate([r_idx[:, None] * inv, c_idx[:, None] * inv], axis=-1)
    cos, sin = jnp.cos(ang), jnp.sin(ang)
    return jnp.concatenate([cos, cos], axis=-1), jnp.concatenate([-sin, sin], axis=-1)


def _mix_body(x_ref, xp_ref, xn_ref, gain_ref, sh_ref, sc_ref, mu_ref, *o_refs):
    i = pl.program_id(0)
    tm = x_ref.shape[0]
    n_ctx_blocks = _n_ctx() // tm
    per_seq = DEC_SEQ // tm
    j = (i - n_ctx_blocks) % per_seq
    is_first = jnp.logical_or(i < n_ctx_blocks, j == 0)
    is_last = jnp.logical_or(i < n_ctx_blocks, j == per_seq - 1)
    gain, sh, sc = gain_ref[...], sh_ref[...], sc_ref[...]
    h = _modulated(x_ref[...], gain, sh, sc)
    h_prev = jnp.where(is_first, 0.0, _modulated(xp_ref[7:8, :], gain, sh, sc))
    h_next = jnp.where(is_last, 0.0, _modulated(xn_ref[0:1, :], gain, sh, sc))
    row = lax.broadcasted_iota(jnp.int32, h.shape, 0)
    prev = jnp.where(row == 0, h_prev, pltpu.roll(h, 1, 0))
    nxt = jnp.where(row == tm - 1, h_next, pltpu.roll(h, tm - 1, 0))
    xx = 0.5 * (prev + nxt) - h
    for n, o_ref in enumerate(o_refs):
        o_ref[...] = (h + xx * mu_ref[n:n + 1, :]).astype(o_ref.dtype)


def rwkv_mix(x, gain, shift, scale, mu):
    n, d = x.shape
    tm = ROW_BLOCK
    rb = tm // 8
    last8 = n // 8 - 1
    return pl.pallas_call(
        _mix_body,
        grid=(n // tm,),
        in_specs=[pl.BlockSpec((tm, d), lambda i: (i, 0)),
                  pl.BlockSpec((8, d), lambda i: (jnp.maximum(i * rb - 1, 0), 0)),
                  pl.BlockSpec((8, d), lambda i: (jnp.minimum((i + 1) * rb, last8), 0)),
                  pl.BlockSpec((1, d), lambda i: (0, 0)),
                  _group_spec(d, tm), _group_spec(d, tm),
                  pl.BlockSpec((mu.shape[0], d), lambda i: (0, 0))],
        out_specs=[pl.BlockSpec((tm, d), lambda i: (i, 0)) for _ in range(6)],
        out_shape=[jax.ShapeDtypeStruct((n, d), BF16) for _ in range(6)],
        compiler_params=_params("arbitrary"),
        name="rwkv_mix",
    )(x, x, x, gain.reshape(1, d), shift, scale, mu)


def _seg_sum(x, ones_bd):
    hi, lo = _split_bf16(x)
    return (jnp.dot(hi, ones_bd, preferred_element_type=F32)
            + jnp.dot(lo, ones_bd, preferred_element_type=F32))


def _scan_body(r_ref, k_ref, v_ref, wl_ref, al_ref, g_ref, kk_ref, ka_ref, rk_ref, lnw_ref, lnb_ref,
               s0_ref, *rest, reverse, add_prev):
    if add_prev:
        prev_ref, o_ref, sfin_ref, s_scr = rest
    else:
        o_ref, sfin_ref, s_scr = rest
    L = SCAN_CHUNK
    R = 2 * L
    hw = RWKV_HEAD
    n_pairs = s_scr.shape[0]
    n_il = min(SCAN_INTERLEAVE, n_pairs)
    step = pl.program_id(0)
    n_chunks = pl.num_programs(0)
    c = (n_chunks - 1 - step) if reverse else step
    ctx_chunks = _n_ctx() // L
    per_ctx, per_lat = SEQ // L, DEC_SEQ // L
    pos = jnp.where(c < ctx_chunks, c % per_ctx, (c - ctx_chunks) % per_lat)
    seq_len = jnp.where(c < ctx_chunks, per_ctx, per_lat)
    first_pos = seq_len - 1 if reverse else 0
    last_pos = 0 if reverse else seq_len - 1

    @pl.when(pos == first_pos)
    def _():
        s_scr[...] = s0_ref[...]

    ri = lax.broadcasted_iota(jnp.int32, (L, L), 0)
    ci = lax.broadcasted_iota(jnp.int32, (L, L), 1)
    before = (ci > ri) if reverse else (ci < ri)
    incl_bf = jnp.logical_or(before, ci == ri).astype(BF16)
    r4 = lax.broadcasted_iota(jnp.int32, (2 * R, 2 * R), 0)
    c4 = lax.broadcasted_iota(jnp.int32, (2 * R, 2 * R), 1)
    same_head = ((r4 % R) // L) == ((c4 % R) // L)
    t4, j4 = r4 % L, c4 % L
    before4 = (j4 > t4) if reverse else (j4 < t4)
    mask4 = jnp.logical_and(same_head, jnp.logical_or(before4, jnp.logical_and(r4 >= R, j4 == t4)))
    eye = (lax.broadcasted_iota(jnp.int32, (R, R), 0) == lax.broadcasted_iota(jnp.int32, (R, R), 1)).astype(F32)
    lr = lax.broadcasted_iota(jnp.int32, (LANES, LANES), 0)
    lc = lax.broadcasted_iota(jnp.int32, (LANES, LANES), 1)
    ones_bd = ((lr // hw) == (lc // hw)).astype(BF16)
    head_a = lax.broadcasted_iota(jnp.int32, (L, LANES), 1) < hw
    nt = (((1,), (1,)), ((), ()))
    inv_hw = 1.0 / hw
    Q = range(n_il)

    def stack(x):
        return jnp.concatenate([jnp.where(head_a, x, 0.0), jnp.where(head_a, 0.0, x)], axis=0)

    def mm(x, y):
        return jnp.dot(x.astype(BF16), y.astype(BF16), preferred_element_type=F32)

    def group(gi, carry):
        sls = [pl.ds(pl.multiple_of((gi * n_il + q) * LANES, LANES), LANES) for q in Q]
        s_vk = [s_scr[gi * n_il + q] for q in Q]
        r = [r_ref[:, sl] for sl in sls]
        k = [k_ref[:, sl] for sl in sls]
        v = [v_ref[:, sl] for sl in sls]
        lw = [DECAY_SCALE * jax.nn.sigmoid(wl_ref[:, sl]) for sl in sls]
        ag = [jax.nn.sigmoid(al_ref[:, sl]) for sl in sls]
        kkf = [k[q] * kk_ref[:, sls[q]] for q in Q]
        nrm = [_seg_sum(kkf[q] * kkf[q], ones_bd) for q in Q]
        kk = [kkf[q] / jnp.maximum(jnp.sqrt(nrm[q]), 1e-12) for q in Q]
        kd = [k[q] * (1.0 + (ag[q] - 1.0) * ka_ref[:, sls[q]]) for q in Q]
        cs = []
        for q in Q:
            w_hi = lw[q].astype(BF16)
            w_r1 = lw[q] - w_hi.astype(F32)
            w_mid = w_r1.astype(BF16)
            w_lo = (w_r1 - w_mid.astype(F32)).astype(BF16)
            cs.append(jnp.dot(incl_bf, w_hi, preferred_element_type=F32)
                      + jnp.dot(incl_bf, w_mid, preferred_element_type=F32)
                      + jnp.dot(incl_bf, w_lo, preferred_element_type=F32))
        e_neg = [jnp.exp(-cs[q]) for q in Q]
        ar = [jnp.concatenate([stack(jnp.exp(cs[q] - lw[q]) * -kk[q]), stack(jnp.exp(cs[q]) * r[q])],
                              axis=0).astype(BF16) for q in Q]
        bk = [jnp.concatenate([stack(e_neg[q] * kk[q] * ag[q]), stack(e_neg[q] * kd[q])],
                              axis=0).astype(BF16) for q in Q]
        v_s = [stack(v[q]).astype(BF16) for q in Q]
        gm = [jnp.where(mask4, lax.dot_general(ar[q], bk[q], nt, preferred_element_type=F32), 0.0) for q in Q]
        a_s = [lax.dot_general(ar[q], s_vk[q].astype(BF16), nt, preferred_element_type=F32) for q in Q]
        w0 = [a_s[q][:R] + mm(gm[q][:R, R:], v_s[q]) for q in Q]
        m = [eye + gm[q][:R, :R] for q in Q]
        pw = [gm[q][:R, :R] for q in Q]
        n = 1
        while n < L // 2:
            pw = [mm(pw[q], pw[q]) for q in Q]
            m = [m[q] + mm(m[q], pw[q]) for q in Q]
            n *= 2
        u_s = [mm(m[q], w0[q]) for q in Q]
        uv = [jnp.concatenate([u_s[q], v_s[q].astype(F32)], axis=0) for q in Q]
        y_s = [a_s[q][R:] + mm(gm[q][R:, :], uv[q]) for q in Q]
        y = [y_s[q][:L] + y_s[q][L:] for q in Q]
        upd = [mm(uv[q].T, bk[q]) for q in Q]
        for q in Q:
            c_tot = cs[q][0:1, :] if reverse else cs[q][L - 1:L, :]
            s_scr[gi * n_il + q] = (s_vk[q] + upd[q]) * jnp.exp(c_tot)
        mean = [_seg_sum(y[q], ones_bd) * inv_hw for q in Q]
        dy = [y[q] - mean[q] for q in Q]
        var = [_seg_sum(dy[q] * dy[q], ones_bd) * inv_hw for q in Q]
        bonus = [_seg_sum(r[q] * kd[q] * rk_ref[:, sls[q]], ones_bd) * v[q] for q in Q]
        for q in Q:
            sl = sls[q]
            yn = dy[q] * lax.rsqrt(var[q] + GN_EPS) * lnw_ref[:, sl] + lnb_ref[:, sl]
            out = (yn + bonus[q]) * g_ref[:, sl]
            if add_prev:
                out = out + prev_ref[:, sl].astype(F32)
            o_ref[:, sl] = out.astype(o_ref.dtype)
        return carry

    lax.fori_loop(0, n_pairs // n_il, group, 0)

    @pl.when(pos == last_pos)
    def _():
        sfin_ref[...] = s_scr[...]


def wkv_scan_dir(r, k, v, wl, al, g, k_k, k_a, r_k, lnx_w, lnx_b, s0, reverse, prev=None):
    n, d = r.shape
    L = SCAN_CHUNK
    n_chunks = n // L
    n_pairs = d // LANES
    ctx_chunks = _n_ctx() // L

    def cidx(s):
        return (n_chunks - 1 - s) if reverse else s

    def seq_of(s):
        c = cidx(s)
        return jnp.where(c < ctx_chunks, c // (SEQ // L), BATCH + (c - ctx_chunks) // (DEC_SEQ // L))

    tok = pl.BlockSpec((L, d), lambda s: (cidx(s), 0))
    vec = pl.BlockSpec((1, d), lambda s: (0, 0))
    st = pl.BlockSpec((None, n_pairs, LANES, LANES), lambda s: (seq_of(s), 0, 0, 0))
    add_prev = prev is not None
    body = functools.partial(_scan_body, reverse=reverse, add_prev=add_prev)
    args = [r, k, v, wl, al, g, k_k.reshape(1, d), k_a.reshape(1, d), r_k.reshape(1, d),
            lnx_w.reshape(1, d), lnx_b.reshape(1, d), s0]
    in_specs = [tok] * 6 + [vec] * 5 + [st]
    if add_prev:
        args.append(prev)
        in_specs.append(tok)
    return pl.pallas_call(
        body,
        grid=(n_chunks,),
        in_specs=in_specs,
        out_specs=[tok, st],
        out_shape=[jax.ShapeDtypeStruct((n, d), BF16 if add_prev else F32),
                   jax.ShapeDtypeStruct(s0.shape, F32)],
        scratch_shapes=[pltpu.VMEM((n_pairs, LANES, LANES), F32)],
        compiler_params=_params("arbitrary"),
        name="wkv_bwd" if reverse else "wkv_fwd",
    )(*args)


def _pair_states(s):
    n_seq, h, hv, hk = s.shape
    s = s.reshape(n_seq, h // 2, 2, hv, hk)
    z = jnp.zeros_like(s[:, :, 0])
    top = jnp.concatenate([s[:, :, 0], z], axis=-1)
    bot = jnp.concatenate([z, s[:, :, 1]], axis=-1)
    return jnp.concatenate([top, bot], axis=-2)


def _unpair_states(s):
    n_seq, n_pairs = s.shape[:2]
    hw = RWKV_HEAD
    a = s[:, :, :hw, :hw]
    b = s[:, :, hw:, hw:]
    return jnp.stack([a, b], axis=2).reshape(n_seq, 2 * n_pairs, hw, hw)


def _expert_up_body(be_ref, first_ref, used_ref, x_ref, wg_ref, wu_ref, o_ref, wg_bf, wu_bf):
    del be_ref
    b = pl.program_id(1)

    @pl.when(b < used_ref[0])
    def _():
        @pl.when(first_ref[b] == 1)
        def _():
            wg_bf[...] = wg_ref[...].astype(BF16)
            wu_bf[...] = wu_ref[...].astype(BF16)

        xp = x_ref[...]
        half = xp.shape[1]
        x_lo = lax.bitcast_convert_type(xp << 16, F32).astype(BF16)
        x_hi = lax.bitcast_convert_type(xp & _UPPER_HALF, F32).astype(BF16)
        gate = (jnp.dot(x_lo, wg_bf[:half, :], preferred_element_type=F32)
                + jnp.dot(x_hi, wg_bf[half:, :], preferred_element_type=F32))
        up = (jnp.dot(x_lo, wu_bf[:half, :], preferred_element_type=F32)
              + jnp.dot(x_hi, wu_bf[half:, :], preferred_element_type=F32))
        o_ref[...] = (_silu(gate) * up).astype(o_ref.dtype)


def expert_up(xs, w_gate, w_up, layer, block_e, first, used, tf=512):
    n_slots = xs.shape[0]
    d = w_gate.shape[2]
    half = d // 2
    n_exp = w_gate.shape[1]
    ff = w_gate.shape[-1]
    w_gate = w_gate.reshape((-1,) + w_gate.shape[2:])
    w_up = w_up.reshape((-1,) + w_up.shape[2:])
    bm = ROW_BLOCK
    tf = min(tf, ff)
    n_blocks = n_slots // bm

    def row(f, b, be, fi, us):
        return jnp.minimum(b, us[0] - 1)

    return pl.pallas_call(
        _expert_up_body,
        grid_spec=pltpu.PrefetchScalarGridSpec(
            num_scalar_prefetch=3,
            grid=(ff // tf, n_blocks),
            in_specs=[pl.BlockSpec((bm, half), lambda f, b, be, fi, us: (row(f, b, be, fi, us), 0)),
                      pl.BlockSpec((None, d, tf), lambda f, b, be, fi, us: (layer * n_exp + be[row(f, b, be, fi, us)], 0, f)),
                      pl.BlockSpec((None, d, tf), lambda f, b, be, fi, us: (layer * n_exp + be[row(f, b, be, fi, us)], 0, f))],
            out_specs=pl.BlockSpec((bm, tf), lambda f, b, be, fi, us: (row(f, b, be, fi, us), f)),
            scratch_shapes=[pltpu.VMEM((d, tf), BF16), pltpu.VMEM((d, tf), BF16)]),
        out_shape=jax.ShapeDtypeStruct((n_slots, ff), BF16),
        compiler_params=_params("arbitrary", "arbitrary"),
        name="expert_up",
    )(block_e, first, used, xs, w_gate, w_up)


def _expert_down_body(be_ref, first_ref, used_ref, h_ref, wd_ref, o_ref, wd_bf):
    del be_ref
    b = pl.program_id(1)

    @pl.when(b < used_ref[0])
    def _():
        @pl.when(first_ref[b] == 1)
        def _():
            wd_bf[...] = wd_ref[...].astype(BF16)

        o_ref[...] = jnp.dot(h_ref[...], wd_bf[...], preferred_element_type=F32).astype(o_ref.dtype)


def expert_down(hm, w_down, layer, block_e, first, used, tn=2048):
    n_slots, ff = hm.shape
    d = w_down.shape[-1]
    n_exp = w_down.shape[1]
    w_down = w_down.reshape((-1,) + w_down.shape[2:])
    bm = ROW_BLOCK
    tn = min(tn, d)
    n_blocks = n_slots // bm

    def row(j, b, be, fi, us):
        return jnp.minimum(b, us[0] - 1)

    return pl.pallas_call(
        _expert_down_body,
        grid_spec=pltpu.PrefetchScalarGridSpec(
            num_scalar_prefetch=3,
            grid=(d // tn, n_blocks),
            in_specs=[pl.BlockSpec((bm, ff), lambda j, b, be, fi, us: (row(j, b, be, fi, us), 0)),
                      pl.BlockSpec((None, ff, tn), lambda j, b, be, fi, us: (layer * n_exp + be[row(j, b, be, fi, us)], 0, j))],
            out_specs=pl.BlockSpec((bm, tn), lambda j, b, be, fi, us: (row(j, b, be, fi, us), j)),
            scratch_shapes=[pltpu.VMEM((ff, tn), BF16)]),
        out_shape=jax.ShapeDtypeStruct((n_slots, d), F32),
        compiler_params=_params("arbitrary", "arbitrary"),
        name="expert_down",
    )(block_e, first, used, hm, w_down)


def route(scores, b_router):
    n = scores.shape[0]
    bm = ROW_BLOCK
    _, idx = lax.top_k(scores + b_router.astype(F32), TOP_K)
    sel = jnp.take_along_axis(scores, idx, axis=-1)
    gw = sel / jnp.sum(sel, axis=-1, keepdims=True) * ROUTED_SCALE
    a = n * TOP_K
    n_blocks = a // bm + N_EXPERTS
    e_flat = idx.reshape(-1)
    onehot = (e_flat[:, None] == jnp.arange(N_EXPERTS)[None, :]).astype(jnp.int32)
    rank = jnp.take_along_axis(jnp.cumsum(onehot, axis=0), e_flat[:, None], axis=1)[:, 0] - 1
    counts = jnp.sum(onehot, axis=0)
    padded = (counts + bm - 1) // bm * bm
    pad_end = jnp.cumsum(padded)
    pad_start = pad_end - padded
    dest = (pad_start[e_flat] + rank).astype(jnp.int32)
    tok_flat = jnp.arange(a, dtype=jnp.int32) // TOP_K
    slot_tok = jnp.full((n_blocks * bm,), n, jnp.int32).at[dest].set(tok_flat)
    block_e = jnp.minimum(jnp.searchsorted(pad_end, jnp.arange(n_blocks) * bm, side="right"),
                          N_EXPERTS - 1).astype(jnp.int32)
    first = jnp.concatenate([jnp.ones((1,), jnp.int32),
                             (block_e[1:] != block_e[:-1]).astype(jnp.int32)])
    used = (pad_end[-1:] // bm).astype(jnp.int32)
    return gw, dest.reshape(n, TOP_K), slot_tok, block_e, first, used


def kernel(x_prompt, x_sample, c, cache_k, cache_v, state_wkv, c_ctx, w_ada, b_ada, norm_mix, norm_ffn, attn_w_qkv, attn_w_o, attn_q_gain, attn_k_gain, rw_mu, rw_w_r, rw_w_k, rw_w_v, rw_w_o, rw_w0, rw_w1, rw_w2, rw_a0, rw_a1, rw_a2, rw_g1, rw_g2, rw_k_k, rw_k_a, rw_r_k, rw_lnx_w, rw_lnx_b, moe_w_router, moe_b_router, moe_w_gate, moe_w_up, moe_w_down, sh_w_gate, sh_w_up, sh_w_down):
    d = D_MODEL
    n_ctx, n = _n_ctx(), _n_tok()
    n_grp = 1 + DEC_BATCH
    x = jnp.concatenate([x_prompt.reshape(n_ctx, d), x_sample.reshape(n - n_ctx, d)], axis=0)

    cvec8 = jnp.zeros((8, d), F32).at[0].set(c_ctx).at[1:n_grp].set(c)
    mods = ada_all(cvec8, w_ada, b_ada)

    def mod(layer, m):
        return mods[layer, :n_grp, m * d:(m + 1) * d].reshape(n_grp, 1, d)

    tm, tn = min(MM_TM, n), min(MM_TN, d)
    resid_gate = lambda acc, res, gate: res + gate * acc

    new_k = new_v = new_s = None
    for i in range(DEPTH):
        j = i // 2
        sh1, sc1, g1, sh2, sc2, g2 = [mod(i, m) for m in range(N_MOD)]
        if i % 2 == 0:
            h = modulate(x, norm_mix[i], sh1, sc1)
            qkv = matmul("qkv", h, [_stacked(attn_w_qkv, j)])
            o_ctx, kc, vc = attention_context(qkv, attn_q_gain[j], attn_k_gain[j])
            cos_f, sin_f = rope_tables()
            o = attention_latent(qkv, jnp.swapaxes(cache_k[:, j], 1, 2), jnp.swapaxes(cache_v[:, j], 1, 2),
                                 cos_f, sin_f, attn_q_gain[j], attn_k_gain[j], o_ctx)
            new_k = kc.reshape(BATCH, 1, SEQ, N_KV_HEADS, HEAD_DIM)
            new_v = vc.reshape(BATCH, 1, SEQ, N_KV_HEADS, HEAD_DIM)
            w_o = _stacked(attn_w_o, j)
        else:
            xr, xw, xk, xv, xa, xg = rwkv_mix(x, norm_mix[i], sh1, sc1, rw_mu[j])
            r = matmul("rwkv_r", xr, [_stacked(rw_w_r, j)])
            k = matmul("rwkv_k", xk, [_stacked(rw_w_k, j)])
            v = matmul("rwkv_v", xv, [_stacked(rw_w_v, j)])
            lw = DECAY_LORA
            la = AAA_LORA
            lg = GATE_LORA
            lgp = -(-lg // LANES) * LANES
            w_mid = matmul("decay_lora_in", xw, [(jnp.concatenate([rw_w1[j, 0], rw_w1[j, 1]], axis=1)[None], 0)],
                           epilogue=jnp.tanh, out_dtype=BF16)
            a_mid = matmul("rate_lora_in", xa, [(jnp.concatenate([rw_a1[j, 0], rw_a1[j, 1]], axis=1)[None], 0)],
                           out_dtype=BF16)
            g1p = jnp.pad(rw_g1[j], ((0, 0), (0, 0), (0, lgp - lg)))
            g2p = jnp.pad(rw_g2[j], ((0, 0), (0, lgp - lg), (0, 0)))
            g_mid = matmul("gate_lora_in", xg, [(jnp.concatenate([g1p[0], g1p[1]], axis=1)[None], 0)],
                           epilogue=jax.nn.sigmoid, out_dtype=BF16)
            add_vec = lambda acc, vec: acc + vec
            s0_all = jnp.concatenate(
                [jnp.zeros((BATCH, 2) + state_wkv.shape[3:], F32), state_wkv[:, j]], axis=0)
            o = None
            finals = []
            for dr in range(2):
                wl = matmul("decay_lora_out", w_mid[:, dr * lw:(dr + 1) * lw], [_stacked(rw_w2, j, dr)],
                            epilogue=add_vec, extra=[(rw_w0[j, dr].reshape(1, d), _row_vec_spec(tn))])
                al = matmul("rate_lora_out", a_mid[:, dr * la:(dr + 1) * la], [_stacked(rw_a2, j, dr)],
                            epilogue=add_vec, extra=[(rw_a0[j, dr].reshape(1, d), _row_vec_spec(tn))])
                gg = matmul("gate_lora_out", g_mid[:, dr * lgp:(dr + 1) * lgp], [(g2p, dr)])
                o, s_fin = wkv_scan_dir(r, k, v, wl, al, gg, rw_k_k[j], rw_k_a[j], rw_r_k[j, dr],
                                        rw_lnx_w[j], rw_lnx_b[j], _pair_states(s0_all[:, dr]),
                                        reverse=(dr == 1), prev=o)
                finals.append(_unpair_states(s_fin[:BATCH]))
            new_s = jnp.stack(finals, axis=1)[:, None]
            w_o = _stacked(rw_w_o, j)
        x = matmul("mixer_out", o, [w_o], epilogue=resid_gate,
                   extra=[(x, _tile_spec(tm, tn)), (g1, _gate_spec(tm, tn))])

        h2, h2_packed, scores = modulate_and_route(x, norm_ffn[i], sh2, sc2, moe_w_router[i])
        gw, dest, slot_tok, block_e, first, used = route(scores, moe_b_router[i])
        packed_pad = jnp.concatenate([h2_packed, jnp.zeros((1, h2_packed.shape[1]), jnp.int32)], axis=0)
        xs = jnp.take(packed_pad, slot_tok, axis=0, mode="clip")
        hm = expert_up(xs, moe_w_gate, moe_w_up, i, block_e, first, used)
        yb = expert_down(hm, moe_w_down, i, block_e, first, used)
        routed = jnp.sum(jnp.take(yb, dest, axis=0, mode="clip") * gw[:, :, None], axis=1)
        hs = matmul("shared_up", h2, [_stacked(sh_w_gate, i), _stacked(sh_w_up, i)],
                    epilogue=lambda a, b: _silu(a) * b, out_dtype=BF16, tn=256)
        moe_out = lambda acc, rt, res, gate: res + gate * (acc + rt)
        x = matmul("moe_out", hs, [_stacked(sh_w_down, i)], epilogue=moe_out,
                   extra=[(routed, _tile_spec(tm, tn)), (x, _tile_spec(tm, tn)), (g2, _gate_spec(tm, tn))])

    y_prompt = x[:n_ctx].reshape(BATCH, SEQ, d)
    y_sample = x[n_ctx:].reshape(DEC_BATCH, DEC_SEQ, d)
    return (y_prompt, y_sample, new_k, new_v, new_s)
```

```python
import functools

import jax
import jax.numpy as jnp
from jax import lax
from jax.experimental import pallas as pl
from jax.experimental.pallas import tpu as pltpu

D_MODEL = 4096
BATCH = 16
SEQ = 256
DEPTH = 2
DEC_BATCH = 2
DEC_SEQ = 1024
PAST_LEN = 512
GRID_W = 64
N_HEADS = 32
N_KV_HEADS = 8
HEAD_DIM = 128
ROPE_THETA = 10000.0
RWKV_HEAD = 64
DECAY_LORA = 128
AAA_LORA = 128
GATE_LORA = 480
N_EXPERTS = 64
TOP_K = 8
EXPERT_FF = 1024
SHARED_FF = 1024
ROUTED_SCALE = 2.5
N_MOD = 6
NORM_EPS = 1e-6
GN_EPS = 64e-5
DECAY_SCALE = -0.606531

F32 = jnp.float32
BF16 = jnp.bfloat16

V7X_VMEM_LIMIT_BYTES = 56 * 1024 * 1024
LANES = 128
ROW_BLOCK = 256
MM_TM = 1024
MM_TN = 512
SCAN_CHUNK = 64
SCAN_INTERLEAVE = 8
ATTN_TQ = 512


def _params(*sem):
    return pltpu.CompilerParams(dimension_semantics=sem, vmem_limit_bytes=V7X_VMEM_LIMIT_BYTES)


def _n_ctx():
    return BATCH * SEQ


def _n_tok():
    return BATCH * SEQ + DEC_BATCH * DEC_SEQ


def _group_of_block(i, tm):
    n_ctx_blocks = _n_ctx() // tm
    return jnp.where(i < n_ctx_blocks, 0, 1 + (i - n_ctx_blocks) // (DEC_SEQ // tm))


def _silu(x):
    return x * jax.nn.sigmoid(x)


def _ada_body(c_ref, w_ref, b_ref, o_ref):
    s = _silu(c_ref[...]).astype(BF16)
    o_ref[...] = jnp.dot(s, w_ref[...].astype(BF16), preferred_element_type=F32) + b_ref[...]


def ada_all(cvec8, w_ada, b_ada, tn=512):
    depth, d, n6 = w_ada.shape
    return pl.pallas_call(
        _ada_body,
        grid=(depth, n6 // tn),
        in_specs=[pl.BlockSpec((8, d), lambda l, j: (0, 0)),
                  pl.BlockSpec((None, d, tn), lambda l, j: (l, 0, j)),
                  pl.BlockSpec((None, 1, tn), lambda l, j: (l, 0, j))],
        out_specs=pl.BlockSpec((None, 8, tn), lambda l, j: (l, 0, j)),
        out_shape=jax.ShapeDtypeStruct((depth, 8, n6), F32),
        compiler_params=_params("arbitrary", "arbitrary"),
        name="ada",
    )(cvec8, w_ada, b_ada.reshape(depth, 1, n6))


def _modulated(x, gain, shift, scale):
    y = x * lax.rsqrt(jnp.mean(x * x, axis=-1, keepdims=True) + NORM_EPS) * gain
    return y * (1.0 + scale) + shift


def _modulate_body(x_ref, gain_ref, sh_ref, sc_ref, o_ref):
    o_ref[...] = _modulated(x_ref[...], gain_ref[...], sh_ref[...], sc_ref[...]).astype(o_ref.dtype)


def _group_spec(d, tm):
    return pl.BlockSpec((None, 1, d), lambda i: (_group_of_block(i, tm), 0, 0))


def modulate(x, gain, shift, scale):
    n, d = x.shape
    tm = ROW_BLOCK
    return pl.pallas_call(
        _modulate_body,
        grid=(n // tm,),
        in_specs=[pl.BlockSpec((tm, d), lambda i: (i, 0)),
                  pl.BlockSpec((1, d), lambda i: (0, 0)),
                  _group_spec(d, tm), _group_spec(d, tm)],
        out_specs=pl.BlockSpec((tm, d), lambda i: (i, 0)),
        out_shape=jax.ShapeDtypeStruct((n, d), BF16),
        compiler_params=_params("arbitrary"),
        name="modulate",
    )(x, gain.reshape(1, d), shift, scale)


def _split_bf16(x):
    hi = x.astype(BF16)
    lo = (x - hi.astype(F32)).astype(BF16)
    return hi, lo


def _bf16_bits(x):
    return lax.bitcast_convert_type(x.astype(BF16).astype(F32), jnp.int32)


_UPPER_HALF = -65536


def _packed_row_words(d):
    return d // 2 + d // 4


def _router_body(x_ref, gain_ref, sh_ref, sc_ref, wr_ref, o_ref, p_ref, s_ref):
    h = _modulated(x_ref[...], gain_ref[...], sh_ref[...], sc_ref[...])
    o_ref[...] = h.astype(BF16)
    half = h.shape[1] // 2
    p_ref[:, :half] = (lax.shift_right_logical(_bf16_bits(h[:, :half]), 16)
                       | (_bf16_bits(h[:, half:]) & _UPPER_HALF))
    p_ref[:, half:] = jnp.zeros((h.shape[0], p_ref.shape[1] - half), jnp.int32)
    h_hi, h_lo = _split_bf16(h)
    w_hi, w_lo = _split_bf16(wr_ref[...])
    logits = (jnp.dot(h_hi, w_hi, preferred_element_type=F32)
              + jnp.dot(h_lo, w_hi, preferred_element_type=F32)
              + jnp.dot(h_hi, w_lo, preferred_element_type=F32))
    s_ref[...] = jax.nn.sigmoid(logits)


def modulate_and_route(x, gain, shift, scale, w_router):
    n, d = x.shape
    e = w_router.shape[1]
    tm = ROW_BLOCK
    return pl.pallas_call(
        _router_body,
        grid=(n // tm,),
        in_specs=[pl.BlockSpec((tm, d), lambda i: (i, 0)),
                  pl.BlockSpec((1, d), lambda i: (0, 0)),
                  _group_spec(d, tm), _group_spec(d, tm),
                  pl.BlockSpec((d, e), lambda i: (0, 0))],
        out_specs=[pl.BlockSpec((tm, d), lambda i: (i, 0)),
                   pl.BlockSpec((tm, _packed_row_words(d)), lambda i: (i, 0)),
                   pl.BlockSpec((tm, e), lambda i: (i, 0))],
        out_shape=[jax.ShapeDtypeStruct((n, d), BF16), jax.ShapeDtypeStruct((n, _packed_row_words(d)), jnp.int32),
                   jax.ShapeDtypeStruct((n, e), F32)],
        compiler_params=_params("arbitrary"),
        name="modulate_route",
    )(x, gain.reshape(1, d), shift, scale, w_router)


def _mm_body(*refs, n_w, n_extra, epilogue):
    a_ref = refs[0]
    w_refs = refs[1:1 + n_w]
    extra = refs[1 + n_w:1 + n_w + n_extra]
    o_ref = refs[1 + n_w + n_extra]
    w_bf = refs[2 + n_w + n_extra:]

    @pl.when(pl.program_id(1) == 0)
    def _():
        for w_ref, w_s in zip(w_refs, w_bf):
            w_s[...] = w_ref[...].astype(BF16)

    a = a_ref[...]
    accs = [jnp.dot(a, w_s[...], preferred_element_type=F32) for w_s in w_bf]
    o_ref[...] = epilogue(*accs, *[r[...] for r in extra]).astype(o_ref.dtype)


def _stacked(w, *lead):
    flat = 0
    for size, idx in zip(w.shape[:len(lead)], lead):
        flat = flat * size + idx
    return w.reshape((-1,) + w.shape[len(lead):]), flat


def matmul(name, a, ws, epilogue=None, extra=(), out_dtype=F32, tm=None, tn=None):
    m, k = a.shape
    n = ws[0][0].shape[2]
    tm = min(tm or MM_TM, m)
    tn = min(tn or MM_TN, n)
    if epilogue is None:
        epilogue = lambda acc: acc
    body = functools.partial(_mm_body, n_w=len(ws), n_extra=len(extra), epilogue=epilogue)

    def w_spec(idx):
        return pl.BlockSpec((None, k, tn), lambda j, i: (idx, 0, j))

    return pl.pallas_call(
        body,
        grid=(n // tn, m // tm),
        in_specs=([pl.BlockSpec((tm, k), lambda j, i: (i, 0))]
                  + [w_spec(idx) for _, idx in ws]
                  + [spec for _, spec in extra]),
        out_specs=pl.BlockSpec((tm, tn), lambda j, i: (i, j)),
        out_shape=jax.ShapeDtypeStruct((m, n), out_dtype),
        scratch_shapes=[pltpu.VMEM((k, tn), BF16) for _ in ws],
        compiler_params=_params("arbitrary", "arbitrary"),
        name=name,
    )(a, *[w for w, _ in ws], *[arr for arr, _ in extra])


def _row_vec_spec(tn):
    return pl.BlockSpec((1, tn), lambda j, i: (0, j))


def _gate_spec(tm, tn):
    return pl.BlockSpec((None, 1, tn), lambda j, i: (_group_of_block(i, tm), 0, j))


def _tile_spec(tm, tn):
    return pl.BlockSpec((tm, tn), lambda j, i: (i, j))


def _rms_head(x, gain):
    return x * lax.rsqrt(jnp.mean(x * x, axis=-1, keepdims=True) + NORM_EPS) * gain


def _rope(x, cos_f, sin_f):
    return x * cos_f + pltpu.roll(x, HEAD_DIM // 2, 1) * sin_f


def _attn_ctx_body(q_ref, k_ref, v_ref, qg_ref, kg_ref, o_ref, kc_ref, vc_ref):
    kn = _rms_head(k_ref[...], kg_ref[...])
    kc_ref[...] = kn
    v = v_ref[...]
    vc_ref[...] = v
    kb = kn.astype(BF16)
    vb = v.astype(BF16)
    scale = HEAD_DIM ** -0.5
    for h in range(N_HEADS // N_KV_HEADS):
        sl = slice(h * HEAD_DIM, (h + 1) * HEAD_DIM)
        qh = _rms_head(q_ref[:, sl], qg_ref[...]).astype(BF16)
        s = lax.dot_general(qh, kb, (((1,), (1,)), ((), ())), preferred_element_type=F32) * scale
        p = jnp.exp(s - jnp.max(s, axis=-1, keepdims=True))
        l = jnp.sum(p, axis=-1, keepdims=True)
        o = jnp.dot(p.astype(BF16), vb, preferred_element_type=F32) / l
        o_ref[:, sl] = o.astype(o_ref.dtype)


def attention_context(qkv, q_gain, k_gain):
    n = _n_ctx()
    g = N_HEADS // N_KV_HEADS
    hd = HEAD_DIM
    return pl.pallas_call(
        _attn_ctx_body,
        grid=(BATCH, N_KV_HEADS),
        in_specs=[pl.BlockSpec((SEQ, g * hd), lambda b, kh: (b, kh)),
                  pl.BlockSpec((SEQ, hd), lambda b, kh: (b, N_HEADS + kh)),
                  pl.BlockSpec((SEQ, hd), lambda b, kh: (b, N_HEADS + N_KV_HEADS + kh)),
                  pl.BlockSpec((1, hd), lambda b, kh: (0, 0)),
                  pl.BlockSpec((1, hd), lambda b, kh: (0, 0))],
        out_specs=[pl.BlockSpec((SEQ, g * hd), lambda b, kh: (b, kh)),
                   pl.BlockSpec((SEQ, hd), lambda b, kh: (b, kh)),
                   pl.BlockSpec((SEQ, hd), lambda b, kh: (b, kh))],
        out_shape=[jax.ShapeDtypeStruct((n, N_HEADS * hd), BF16),
                   jax.ShapeDtypeStruct((_n_ctx(), N_KV_HEADS * hd), F32),
                   jax.ShapeDtypeStruct((_n_ctx(), N_KV_HEADS * hd), F32)],
        compiler_params=_params("arbitrary", "arbitrary"),
        name="attn_ctx",
    )(qkv, qkv, qkv, q_gain.reshape(1, hd), k_gain.reshape(1, hd))


def _attn_lat_body(q_ref, k_ref, v_ref, ck_ref, cv_ref, cosq_ref, sinq_ref, cosk_ref, sink_ref,
                   qg_ref, kg_ref, o_ref):
    kn = _rope(_rms_head(k_ref[...], kg_ref[...]), cosk_ref[...], sink_ref[...]).astype(BF16)
    vb = v_ref[...].astype(BF16)
    ckb = ck_ref[...].astype(BF16)
    cvb = cv_ref[...].astype(BF16)
    scale = HEAD_DIM ** -0.5
    nt = (((1,), (1,)), ((), ()))
    for h in range(N_HEADS // N_KV_HEADS):
        sl = slice(h * HEAD_DIM, (h + 1) * HEAD_DIM)
        qh = _rope(_rms_head(q_ref[:, sl], qg_ref[...]), cosq_ref[...], sinq_ref[...]).astype(BF16)
        s_c = lax.dot_general(qh, ckb, nt, preferred_element_type=F32) * scale
        s_l = lax.dot_general(qh, kn, nt, preferred_element_type=F32) * scale
        m = jnp.maximum(jnp.max(s_c, axis=-1, keepdims=True), jnp.max(s_l, axis=-1, keepdims=True))
        p_c = jnp.exp(s_c - m)
        p_l = jnp.exp(s_l - m)
        l = jnp.sum(p_c, axis=-1, keepdims=True) + jnp.sum(p_l, axis=-1, keepdims=True)
        o = (jnp.dot(p_c.astype(BF16), cvb, preferred_element_type=F32)
             + jnp.dot(p_l.astype(BF16), vb, preferred_element_type=F32)) / l
        o_ref[:, sl] = o.astype(o_ref.dtype)


def attention_latent(qkv, cache_k, cache_v, cos_f, sin_f, q_gain, k_gain):
    g = N_HEADS // N_KV_HEADS
    hd = HEAD_DIM
    tq = min(ATTN_TQ, DEC_SEQ)
    nq = DEC_SEQ // tq
    q_row0 = _n_ctx() // tq
    k_row0 = _n_ctx() // DEC_SEQ
    q_map = lambda b, kh, qi: (q_row0 + b * nq + qi, kh)
    return pl.pallas_call(
        _attn_lat_body,
        grid=(DEC_BATCH, N_KV_HEADS, nq),
        in_specs=[pl.BlockSpec((tq, g * hd), q_map),
                  pl.BlockSpec((DEC_SEQ, hd), lambda b, kh, qi: (k_row0 + b, N_HEADS + kh)),
                  pl.BlockSpec((DEC_SEQ, hd), lambda b, kh, qi: (k_row0 + b, N_HEADS + N_KV_HEADS + kh)),
                  pl.BlockSpec((None, None, PAST_LEN, hd), lambda b, kh, qi: (b, kh, 0, 0)),
                  pl.BlockSpec((None, None, PAST_LEN, hd), lambda b, kh, qi: (b, kh, 0, 0)),
                  pl.BlockSpec((tq, hd), lambda b, kh, qi: (qi, 0)),
                  pl.BlockSpec((tq, hd), lambda b, kh, qi: (qi, 0)),
                  pl.BlockSpec((DEC_SEQ, hd), lambda b, kh, qi: (0, 0)),
                  pl.BlockSpec((DEC_SEQ, hd), lambda b, kh, qi: (0, 0)),
                  pl.BlockSpec((1, hd), lambda b, kh, qi: (0, 0)),
                  pl.BlockSpec((1, hd), lambda b, kh, qi: (0, 0))],
        out_specs=pl.BlockSpec((tq, g * hd), lambda b, kh, qi: (b * nq + qi, kh)),
        out_shape=jax.ShapeDtypeStruct((DEC_BATCH * DEC_SEQ, N_HEADS * hd), BF16),
        compiler_params=_params("arbitrary", "arbitrary", "arbitrary"),
        name="attn_lat",
    )(qkv, qkv, qkv, cache_k, cache_v, cos_f, sin_f, cos_f, sin_f,
      q_gain.reshape(1, hd), k_gain.reshape(1, hd))


def rope_tables():
    rows = DEC_SEQ // GRID_W
    r_idx, c_idx = jnp.meshgrid(jnp.arange(rows), jnp.arange(GRID_W), indexing="ij")
    r_idx = r_idx.reshape(-1).astype(F32)
    c_idx = c_idx.reshape(-1).astype(F32)
    n_freq = HEAD_DIM // 4
    inv = ROPE_THETA ** (-jnp.arange(n_freq, dtype=F32) / n_freq)
    ang = jnp.concatenate([r_idx[:, None] * inv, c_idx[:, None] * inv], axis=-1)
    cos, sin = jnp.cos(ang), jnp.sin(ang)
    return jnp.concatenate([cos, cos], axis=-1), jnp.concatenate([-sin, sin], axis=-1)


def _mix_body(x_ref, xp_ref, xn_ref, gain_ref, sh_ref, sc_ref, mu_ref, *o_refs):
    i = pl.program_id(0)
    tm = x_ref.shape[0]
    n_ctx_blocks = _n_ctx() // tm
    per_seq = DEC_SEQ // tm
    j = (i - n_ctx_blocks) % per_seq
    is_first = jnp.logical_or(i < n_ctx_blocks, j == 0)
    is_last = jnp.logical_or(i < n_ctx_blocks, j == per_seq - 1)
    gain, sh, sc = gain_ref[...], sh_ref[...], sc_ref[...]
    h = _modulated(x_ref[...], gain, sh, sc)
    h_prev = jnp.where(is_first, 0.0, _modulated(xp_ref[7:8, :], gain, sh, sc))
    h_next = jnp.where(is_last, 0.0, _modulated(xn_ref[0:1, :], gain, sh, sc))
    row = lax.broadcasted_iota(jnp.int32, h.shape, 0)
    prev = jnp.where(row == 0, h_prev, pltpu.roll(h, 1, 0))
    nxt = jnp.where(row == tm - 1, h_next, pltpu.roll(h, tm - 1, 0))
    xx = 0.5 * (prev + nxt) - h
    for n, o_ref in enumerate(o_refs):
        o_ref[...] = (h + xx * mu_ref[n:n + 1, :]).astype(o_ref.dtype)


def rwkv_mix(x, gain, shift, scale, mu):
    n, d = x.shape
    tm = ROW_BLOCK
    rb = tm // 8
    last8 = n // 8 - 1
    return pl.pallas_call(
        _mix_body,
        grid=(n // tm,),
        in_specs=[pl.BlockSpec((tm, d), lambda i: (i, 0)),
                  pl.BlockSpec((8, d), lambda i: (jnp.maximum(i * rb - 1, 0), 0)),
                  pl.BlockSpec((8, d), lambda i: (jnp.minimum((i + 1) * rb, last8), 0)),
                  pl.BlockSpec((1, d), lambda i: (0, 0)),
                  _group_spec(d, tm), _group_spec(d, tm),
                  pl.BlockSpec((mu.shape[0], d), lambda i: (0, 0))],
        out_specs=[pl.BlockSpec((tm, d), lambda i: (i, 0)) for _ in range(6)],
        out_shape=[jax.ShapeDtypeStruct((n, d), BF16) for _ in range(6)],
        compiler_params=_params("arbitrary"),
        name="rwkv_mix",
    )(x, x, x, gain.reshape(1, d), shift, scale, mu)


def _seg_sum(x, ones_bd):
    hi, lo = _split_bf16(x)
    return (jnp.dot(hi, ones_bd, preferred_element_type=F32)
            + jnp.dot(lo, ones_bd, preferred_element_type=F32))


def _scan_body(r_ref, k_ref, v_ref, wl_ref, al_ref, g_ref, kk_ref, ka_ref, rk_ref, lnw_ref, lnb_ref,
               s0_ref, *rest, reverse, add_prev):
    if add_prev:
        prev_ref, o_ref, sfin_ref, s_scr = rest
    else:
        o_ref, sfin_ref, s_scr = rest
    L = SCAN_CHUNK
    R = 2 * L
    hw = RWKV_HEAD
    n_pairs = s_scr.shape[0]
    n_il = min(SCAN_INTERLEAVE, n_pairs)
    step = pl.program_id(0)
    n_chunks = pl.num_programs(0)
    c = (n_chunks - 1 - step) if reverse else step
    ctx_chunks = _n_ctx() // L
    per_ctx, per_lat = SEQ // L, DEC_SEQ // L
    pos = jnp.where(c < ctx_chunks, c % per_ctx, (c - ctx_chunks) % per_lat)
    seq_len = jnp.where(c < ctx_chunks, per_ctx, per_lat)
    first_pos = seq_len - 1 if reverse else 0
    last_pos = 0 if reverse else seq_len - 1

    @pl.when(pos == first_pos)
    def _():
        s_scr[...] = s0_ref[...]

    ri = lax.broadcasted_iota(jnp.int32, (L, L), 0)
    ci = lax.broadcasted_iota(jnp.int32, (L, L), 1)
    before = (ci > ri) if reverse else (ci < ri)
    incl_bf = jnp.logical_or(before, ci == ri).astype(BF16)
    r4 = lax.broadcasted_iota(jnp.int32, (2 * R, 2 * R), 0)
    c4 = lax.broadcasted_iota(jnp.int32, (2 * R, 2 * R), 1)
    same_head = ((r4 % R) // L) == ((c4 % R) // L)
    t4, j4 = r4 % L, c4 % L
    before4 = (j4 > t4) if reverse else (j4 < t4)
    mask4 = jnp.logical_and(same_head, jnp.logical_or(before4, jnp.logical_and(r4 >= R, j4 == t4)))
    eye = (lax.broadcasted_iota(jnp.int32, (R, R), 0) == lax.broadcasted_iota(jnp.int32, (R, R), 1)).astype(F32)
    lr = lax.broadcasted_iota(jnp.int32, (LANES, LANES), 0)
    lc = lax.broadcasted_iota(jnp.int32, (LANES, LANES), 1)
    ones_bd = ((lr // hw) == (lc // hw)).astype(BF16)
    head_a = lax.broadcasted_iota(jnp.int32, (L, LANES), 1) < hw
    nt = (((1,), (1,)), ((), ()))
    inv_hw = 1.0 / hw
    Q = range(n_il)

    def stack(x):
        return jnp.concatenate([jnp.where(head_a, x, 0.0), jnp.where(head_a, 0.0, x)], axis=0)

    def mm(x, y):
        return jnp.dot(x.astype(BF16), y.astype(BF16), preferred_element_type=F32)

    def group(gi, carry):
        sls = [pl.ds(pl.multiple_of((gi * n_il + q) * LANES, LANES), LANES) for q in Q]
        s_vk = [s_scr[gi * n_il + q] for q in Q]
        r = [r_ref[:, sl] for sl in sls]
        k = [k_ref[:, sl] for sl in sls]
        v = [v_ref[:, sl] for sl in sls]
        lw = [DECAY_SCALE * jax.nn.sigmoid(wl_ref[:, sl]) for sl in sls]
        ag = [jax.nn.sigmoid(al_ref[:, sl]) for sl in sls]
        kkf = [k[q] * kk_ref[:, sls[q]] for q in Q]
        nrm = [_seg_sum(kkf[q] * kkf[q], ones_bd) for q in Q]
        kk = [kkf[q] / jnp.maximum(jnp.sqrt(nrm[q]), 1e-12) for q in Q]
        kd = [k[q] * (1.0 + (ag[q] - 1.0) * ka_ref[:, sls[q]]) for q in Q]
        cs = []
        for q in Q:
            w_hi = lw[q].astype(BF16)
            w_r1 = lw[q] - w_hi.astype(F32)
            w_mid = w_r1.astype(BF16)
            w_lo = (w_r1 - w_mid.astype(F32)).astype(BF16)
            cs.append(jnp.dot(incl_bf, w_hi, preferred_element_type=F32)
                      + jnp.dot(incl_bf, w_mid, preferred_element_type=F32)
                      + jnp.dot(incl_bf, w_lo, preferred_element_type=F32))
        e_neg = [jnp.exp(-cs[q]) for q in Q]
        ar = [jnp.concatenate([stack(jnp.exp(cs[q] - lw[q]) * -kk[q]), stack(jnp.exp(cs[q]) * r[q])],
                              axis=0).astype(BF16) for q in Q]
        bk = [jnp.concatenate([stack(e_neg[q] * kk[q] * ag[q]), stack(e_neg[q] * kd[q])],
                              axis=0).astype(BF16) for q in Q]
        v_s = [stack(v[q]).astype(BF16) for q in Q]
        gm = [jnp.where(mask4, lax.dot_general(ar[q], bk[q], nt, preferred_element_type=F32), 0.0) for q in Q]
        a_s = [lax.dot_general(ar[q], s_vk[q].astype(BF16), nt, preferred_element_type=F32) for q in Q]
        w0 = [a_s[q][:R] + mm(gm[q][:R, R:], v_s[q]) for q in Q]
        m = [eye + gm[q][:R, :R] for q in Q]
        pw = [gm[q][:R, :R] for q in Q]
        n = 1
        while n < L // 2:
            pw = [mm(pw[q], pw[q]) for q in Q]
            m = [m[q] + mm(m[q], pw[q]) for q in Q]
            n *= 2
        u_s = [mm(m[q], w0[q]) for q in Q]
        uv = [jnp.concatenate([u_s[q], v_s[q].astype(F32)], axis=0) for q in Q]
        y_s = [a_s[q][R:] + mm(gm[q][R:, :], uv[q]) for q in Q]
        y = [y_s[q][:L] + y_s[q][L:] for q in Q]
        upd = [mm(uv[q].T, bk[q]) for q in Q]
        for q in Q:
            c_tot = cs[q][0:1, :] if reverse else cs[q][L - 1:L, :]
            s_scr[gi * n_il + q] = (s_vk[q] + upd[q]) * jnp.exp(c_tot)
        mean = [_seg_sum(y[q], ones_bd) * inv_hw for q in Q]
        dy = [y[q] - mean[q] for q in Q]
        var = [_seg_sum(dy[q] * dy[q], ones_bd) * inv_hw for q in Q]
        bonus = [_seg_sum(r[q] * kd[q] * rk_ref[:, sls[q]], ones_bd) * v[q] for q in Q]
        for q in Q:
            sl = sls[q]
            yn = dy[q] * lax.rsqrt(var[q] + GN_EPS) * lnw_ref[:, sl] + lnb_ref[:, sl]
            out = (yn + bonus[q]) * g_ref[:, sl]
            if add_prev:
                out = out + prev_ref[:, sl].astype(F32)
            o_ref[:, sl] = out.astype(o_ref.dtype)
        return carry

    lax.fori_loop(0, n_pairs // n_il, group, 0)

    @pl.when(pos == last_pos)
    def _():
        sfin_ref[...] = s_scr[...]


def wkv_scan_dir(r, k, v, wl, al, g, k_k, k_a, r_k, lnx_w, lnx_b, s0, reverse, prev=None):
    n, d = r.shape
    L = SCAN_CHUNK
    n_chunks = n // L
    n_pairs = d // LANES
    ctx_chunks = _n_ctx() // L

    def cidx(s):
        return (n_chunks - 1 - s) if reverse else s

    def seq_of(s):
        c = cidx(s)
        return jnp.where(c < ctx_chunks, c // (SEQ // L), BATCH + (c - ctx_chunks) // (DEC_SEQ // L))

    tok = pl.BlockSpec((L, d), lambda s: (cidx(s), 0))
    vec = pl.BlockSpec((1, d), lambda s: (0, 0))
    st = pl.BlockSpec((None, n_pairs, LANES, LANES), lambda s: (seq_of(s), 0, 0, 0))
    add_prev = prev is not None
    body = functools.partial(_scan_body, reverse=reverse, add_prev=add_prev)
    args = [r, k, v, wl, al, g, k_k.reshape(1, d), k_a.reshape(1, d), r_k.reshape(1, d),
            lnx_w.reshape(1, d), lnx_b.reshape(1, d), s0]
    in_specs = [tok] * 6 + [vec] * 5 + [st]
    if add_prev:
        args.append(prev)
        in_specs.append(tok)
    return pl.pallas_call(
        body,
        grid=(n_chunks,),
        in_specs=in_specs,
        out_specs=[tok, st],
        out_shape=[jax.ShapeDtypeStruct((n, d), BF16 if add_prev else F32),
                   jax.ShapeDtypeStruct(s0.shape, F32)],
        scratch_shapes=[pltpu.VMEM((n_pairs, LANES, LANES), F32)],
        compiler_params=_params("arbitrary"),
        name="wkv_bwd" if reverse else "wkv_fwd",
    )(*args)


def _pair_states(s):
    n_seq, h, hv, hk = s.shape
    s = s.reshape(n_seq, h // 2, 2, hv, hk)
    z = jnp.zeros_like(s[:, :, 0])
    top = jnp.concatenate([s[:, :, 0], z], axis=-1)
    bot = jnp.concatenate([z, s[:, :, 1]], axis=-1)
    return jnp.concatenate([top, bot], axis=-2)


def _unpair_states(s):
    n_seq, n_pairs = s.shape[:2]
    hw = RWKV_HEAD
    a = s[:, :, :hw, :hw]
    b = s[:, :, hw:, hw:]
    return jnp.stack([a, b], axis=2).reshape(n_seq, 2 * n_pairs, hw, hw)


ITEM_BLOCKS = 4


def _store_rows(o_ref, value):
    rows = value.shape[0]
    o_ref[:rows, :] = value.astype(o_ref.dtype)
    if rows < o_ref.shape[0]:
        o_ref[rows:, :] = jnp.zeros((o_ref.shape[0] - rows, o_ref.shape[1]), o_ref.dtype)


def _expert_up_body(ie_ref, ib_ref, inb_ref, last_ref, *refs):
    del ie_ref, ib_ref, last_ref
    x_refs = refs[:ITEM_BLOCKS]
    wg_ref, wu_ref, o_ref, acc_g, acc_u = refs[ITEM_BLOCKS:]
    it = pl.program_id(1)
    kh = pl.program_id(2)
    nb = inb_ref[it]
    bm = x_refs[0].shape[0]
    for n_sub in range(1, ITEM_BLOCKS + 1):
        @pl.when(nb == n_sub)
        def _(n_sub=n_sub):
            rows = n_sub * bm
            xp = jnp.concatenate([x_refs[s][...] for s in range(n_sub)], axis=0)
            bits = jnp.where(kh == 0, xp << 16, xp & _UPPER_HALF)
            x = lax.bitcast_convert_type(bits, F32).astype(BF16)
            gate = jnp.dot(x, wg_ref[...].astype(BF16), preferred_element_type=F32)
            up = jnp.dot(x, wu_ref[...].astype(BF16), preferred_element_type=F32)

            @pl.when(kh == 0)
            def _():
                acc_g[:rows, :] = gate
                acc_u[:rows, :] = up

            @pl.when(kh == 1)
            def _():
                _store_rows(o_ref, _silu(acc_g[:rows, :] + gate) * (acc_u[:rows, :] + up))


def _item_x_spec(s, bm, half):
    def index(f, it, kh, ie, ib, inb, last):
        return ib[it] + jnp.minimum(s, jnp.maximum(inb[it] - 1, 0)), 0
    return pl.BlockSpec((bm, half), index)


def expert_up(xs, w_gate, w_up, layer, item_e, item_blk, item_nb, last_item, tf=512):
    d = w_gate.shape[2]
    half = d // 2
    n_exp = w_gate.shape[1]
    ff = w_gate.shape[-1]
    w_gate = w_gate.reshape((-1,) + w_gate.shape[2:])
    w_up = w_up.reshape((-1,) + w_up.shape[2:])
    bm = ROW_BLOCK
    tf = min(tf, ff)
    n_items = item_e.shape[0]
    item_rows = ITEM_BLOCKS * bm
    w_spec = pl.BlockSpec((None, half, tf),
                          lambda f, it, kh, ie, ib, inb, last: (layer * n_exp + ie[it], kh, f))
    return pl.pallas_call(
        _expert_up_body,
        grid_spec=pltpu.PrefetchScalarGridSpec(
            num_scalar_prefetch=4,
            grid=(ff // tf, n_items, 2),
            in_specs=[_item_x_spec(s, bm, half) for s in range(ITEM_BLOCKS)] + [w_spec, w_spec],
            out_specs=pl.BlockSpec((item_rows, tf),
                                   lambda f, it, kh, ie, ib, inb, last: (jnp.minimum(it, last[0]), f)),
            scratch_shapes=[pltpu.VMEM((item_rows, tf), F32), pltpu.VMEM((item_rows, tf), F32)]),
        out_shape=jax.ShapeDtypeStruct((n_items * item_rows, ff), BF16),
        compiler_params=_params("arbitrary", "arbitrary", "arbitrary"),
        name="expert_up",
    )(item_e, item_blk, item_nb, last_item, *([xs] * ITEM_BLOCKS), w_gate, w_up)


def _expert_down_body(ie_ref, ib_ref, inb_ref, last_ref, h_ref, wd_ref, o_ref):
    del ie_ref, ib_ref, last_ref
    nb = inb_ref[pl.program_id(1)]
    bm = h_ref.shape[0] // ITEM_BLOCKS
    for n_sub in range(1, ITEM_BLOCKS + 1):
        @pl.when(nb == n_sub)
        def _(n_sub=n_sub):
            y = jnp.dot(h_ref[:n_sub * bm, :], wd_ref[...].astype(BF16), preferred_element_type=F32)
            _store_rows(o_ref, y)


def expert_down(hm, w_down, layer, item_e, item_blk, item_nb, last_item, tn=2048):
    n_rows, ff = hm.shape
    d = w_down.shape[-1]
    n_exp = w_down.shape[1]
    w_down = w_down.reshape((-1,) + w_down.shape[2:])
    tn = min(tn, d)
    n_items = item_e.shape[0]
    item_rows = n_rows // n_items
    return pl.pallas_call(
        _expert_down_body,
        grid_spec=pltpu.PrefetchScalarGridSpec(
            num_scalar_prefetch=4,
            grid=(d // tn, n_items),
            in_specs=[pl.BlockSpec((item_rows, ff), lambda j, it, ie, ib, inb, last: (jnp.minimum(it, last[0]), 0)),
                      pl.BlockSpec((None, ff, tn), lambda j, it, ie, ib, inb, last: (layer * n_exp + ie[it], 0, j))],
            out_specs=pl.BlockSpec((item_rows, tn), lambda j, it, ie, ib, inb, last: (jnp.minimum(it, last[0]), j))),
        out_shape=jax.ShapeDtypeStruct((n_rows, d), F32),
        compiler_params=_params("arbitrary", "arbitrary"),
        name="expert_down",
    )(item_e, item_blk, item_nb, last_item, hm, w_down)


def route(scores, b_router):
    n = scores.shape[0]
    bm = ROW_BLOCK
    _, idx = lax.top_k(scores + b_router.astype(F32), TOP_K)
    sel = jnp.take_along_axis(scores, idx, axis=-1)
    gw = sel / jnp.sum(sel, axis=-1, keepdims=True) * ROUTED_SCALE
    a = n * TOP_K
    n_blocks = a // bm + N_EXPERTS
    e_flat = idx.reshape(-1)
    onehot = (e_flat[:, None] == jnp.arange(N_EXPERTS)[None, :]).astype(jnp.int32)
    rank = jnp.take_along_axis(jnp.cumsum(onehot, axis=0), e_flat[:, None], axis=1)[:, 0] - 1
    counts = jnp.sum(onehot, axis=0)
    padded = (counts + bm - 1) // bm * bm
    pad_end = jnp.cumsum(padded)
    pad_start = pad_end - padded
    dest = (pad_start[e_flat] + rank).astype(jnp.int32)
    tok_flat = jnp.arange(a, dtype=jnp.int32) // TOP_K
    slot_tok = jnp.full((n_blocks * bm,), n, jnp.int32).at[dest].set(tok_flat)
    ib = ITEM_BLOCKS
    n_items = n_blocks // ib + N_EXPERTS
    e_blocks = padded // bm
    e_items = (e_blocks + ib - 1) // ib
    item_end = jnp.cumsum(e_items)
    item_begin = item_end - e_items
    last_item = (item_end[-1:] - 1).astype(jnp.int32)
    it = jnp.minimum(jnp.arange(n_items), last_item[0])
    item_e = jnp.minimum(jnp.searchsorted(item_end, it, side="right"), N_EXPERTS - 1).astype(jnp.int32)
    local = it - item_begin[item_e]
    item_blk = (pad_start[item_e] // bm + ib * local).astype(jnp.int32)
    item_nb = jnp.where(jnp.arange(n_items) <= last_item[0],
                        jnp.clip(e_blocks[item_e] - ib * local, 0, ib), 0).astype(jnp.int32)
    item_rows = ib * bm
    out_row = ((item_begin[e_flat] + rank // item_rows) * item_rows + rank % item_rows).astype(jnp.int32)
    return gw, out_row.reshape(n, TOP_K), slot_tok, item_e, item_blk, item_nb, last_item


def kernel(x_prompt, x_sample, c, cache_k, cache_v, state_wkv, c_ctx, w_ada, b_ada, norm_mix, norm_ffn, attn_w_qkv, attn_w_o, attn_q_gain, attn_k_gain, rw_mu, rw_w_r, rw_w_k, rw_w_v, rw_w_o, rw_w0, rw_w1, rw_w2, rw_a0, rw_a1, rw_a2, rw_g1, rw_g2, rw_k_k, rw_k_a, rw_r_k, rw_lnx_w, rw_lnx_b, moe_w_router, moe_b_router, moe_w_gate, moe_w_up, moe_w_down, sh_w_gate, sh_w_up, sh_w_down):
    d = D_MODEL
    n_ctx, n = _n_ctx(), _n_tok()
    n_grp = 1 + DEC_BATCH
    x = jnp.concatenate([x_prompt.reshape(n_ctx, d), x_sample.reshape(n - n_ctx, d)], axis=0)

    cvec8 = jnp.zeros((8, d), F32).at[0].set(c_ctx).at[1:n_grp].set(c)
    mods = ada_all(cvec8, w_ada, b_ada)

    def mod(layer, m):
        return mods[layer, :n_grp, m * d:(m + 1) * d].reshape(n_grp, 1, d)

    tm, tn = min(MM_TM, n), min(MM_TN, d)
    resid_gate = lambda acc, res, gate: res + gate * acc

    new_k = new_v = new_s = None
    for i in range(DEPTH):
        j = i // 2
        sh1, sc1, g1, sh2, sc2, g2 = [mod(i, m) for m in range(N_MOD)]
        if i % 2 == 0:
            h = modulate(x, norm_mix[i], sh1, sc1)
            qkv = matmul("qkv", h, [_stacked(attn_w_qkv, j)])
            o_ctx, kc, vc = attention_context(qkv, attn_q_gain[j], attn_k_gain[j])
            cos_f, sin_f = rope_tables()
            o_lat = attention_latent(qkv, jnp.swapaxes(cache_k[:, j], 1, 2), jnp.swapaxes(cache_v[:, j], 1, 2),
                                     cos_f, sin_f, attn_q_gain[j], attn_k_gain[j])
            o = jnp.concatenate([o_ctx, o_lat], axis=0)
            new_k = kc.reshape(BATCH, 1, SEQ, N_KV_HEADS, HEAD_DIM)
            new_v = vc.reshape(BATCH, 1, SEQ, N_KV_HEADS, HEAD_DIM)
            w_o = _stacked(attn_w_o, j)
        else:
            xr, xw, xk, xv, xa, xg = rwkv_mix(x, norm_mix[i], sh1, sc1, rw_mu[j])
            r = matmul("rwkv_r", xr, [_stacked(rw_w_r, j)])
            k = matmul("rwkv_k", xk, [_stacked(rw_w_k, j)])
            v = matmul("rwkv_v", xv, [_stacked(rw_w_v, j)])
            lw = DECAY_LORA
            la = AAA_LORA
            lg = GATE_LORA
            lgp = -(-lg // LANES) * LANES
            w_mid = matmul("decay_lora_in", xw, [(jnp.concatenate([rw_w1[j, 0], rw_w1[j, 1]], axis=1)[None], 0)],
                           epilogue=jnp.tanh, out_dtype=BF16)
            a_mid = matmul("rate_lora_in", xa, [(jnp.concatenate([rw_a1[j, 0], rw_a1[j, 1]], axis=1)[None], 0)],
                           out_dtype=BF16)
            g1p = jnp.pad(rw_g1[j], ((0, 0), (0, 0), (0, lgp - lg)))
            g2p = jnp.pad(rw_g2[j], ((0, 0), (0, lgp - lg), (0, 0)))
            g_mid = matmul("gate_lora_in", xg, [(jnp.concatenate([g1p[0], g1p[1]], axis=1)[None], 0)],
                           epilogue=jax.nn.sigmoid, out_dtype=BF16)
            add_vec = lambda acc, vec: acc + vec
            s0_all = jnp.concatenate(
                [jnp.zeros((BATCH, 2) + state_wkv.shape[3:], F32), state_wkv[:, j]], axis=0)
            o = None
            finals = []
            for dr in range(2):
                wl = matmul("decay_lora_out", w_mid[:, dr * lw:(dr + 1) * lw], [_stacked(rw_w2, j, dr)],
                            epilogue=add_vec, extra=[(rw_w0[j, dr].reshape(1, d), _row_vec_spec(tn))])
                al = matmul("rate_lora_out", a_mid[:, dr * la:(dr + 1) * la], [_stacked(rw_a2, j, dr)],
                            epilogue=add_vec, extra=[(rw_a0[j, dr].reshape(1, d), _row_vec_spec(tn))])
                gg = matmul("gate_lora_out", g_mid[:, dr * lgp:(dr + 1) * lgp], [(g2p, dr)])
                o, s_fin = wkv_scan_dir(r, k, v, wl, al, gg, rw_k_k[j], rw_k_a[j], rw_r_k[j, dr],
                                        rw_lnx_w[j], rw_lnx_b[j], _pair_states(s0_all[:, dr]),
                                        reverse=(dr == 1), prev=o)
                finals.append(_unpair_states(s_fin[:BATCH]))
            new_s = jnp.stack(finals, axis=1)[:, None]
            w_o = _stacked(rw_w_o, j)
        x = matmul("mixer_out", o, [w_o], epilogue=resid_gate,
                   extra=[(x, _tile_spec(tm, tn)), (g1, _gate_spec(tm, tn))])

        h2, h2_packed, scores = modulate_and_route(x, norm_ffn[i], sh2, sc2, moe_w_router[i])
        gw, out_row, slot_tok, *items = route(scores, moe_b_router[i])
        packed_pad = jnp.concatenate([h2_packed, jnp.zeros((1, h2_packed.shape[1]), jnp.int32)], axis=0)
        xs = jnp.take(packed_pad, slot_tok, axis=0, mode="clip")
        hm = expert_up(xs, moe_w_gate, moe_w_up, i, *items)
        yb = expert_down(hm, moe_w_down, i, *items)
        routed = jnp.sum(jnp.take(yb, out_row, axis=0, mode="clip") * gw[:, :, None], axis=1)
        hs = matmul("shared_up", h2, [_stacked(sh_w_gate, i), _stacked(sh_w_up, i)],
                    epilogue=lambda a, b: _silu(a) * b, out_dtype=BF16, tn=256)
        moe_out = lambda acc, rt, res, gate: res + gate * (acc + rt)
        x = matmul("moe_out", hs, [_stacked(sh_w_down, i)], epilogue=moe_out,
                   extra=[(routed, _tile_spec(tm, tn)), (x, _tile_spec(tm, tn)), (g2, _gate_spec(tm, tn))])

    y_prompt = x[:n_ctx].reshape(BATCH, SEQ, d)
    y_sample = x[n_ctx:].reshape(DEC_BATCH, DEC_SEQ, d)
    return (y_prompt, y_sample, new_k, new_v, new_s)
```

```python
import functools

import jax
import jax.numpy as jnp
from jax import lax
from jax.experimental import pallas as pl
from jax.experimental.pallas import tpu as pltpu

D_MODEL = 4096
BATCH = 16
SEQ = 256
DEPTH = 2
DEC_BATCH = 2
DEC_SEQ = 1024
PAST_LEN = 512
GRID_W = 64
N_HEADS = 32
N_KV_HEADS = 8
HEAD_DIM = 128
ROPE_THETA = 10000.0
RWKV_HEAD = 64
DECAY_LORA = 128
AAA_LORA = 128
GATE_LORA = 480
N_EXPERTS = 64
TOP_K = 8
EXPERT_FF = 1024
SHARED_FF = 1024
ROUTED_SCALE = 2.5
N_MOD = 6
NORM_EPS = 1e-6
GN_EPS = 64e-5
DECAY_SCALE = -0.606531

F32 = jnp.float32
BF16 = jnp.bfloat16

V7X_VMEM_LIMIT_BYTES = 56 * 1024 * 1024
LANES = 128
ROW_BLOCK = 256
MM_TM = 1024
MM_TN = 512
SCAN_CHUNK = 64
SCAN_INTERLEAVE = 8
ATTN_TQ = 512


def _params(*sem):
    return pltpu.CompilerParams(dimension_semantics=sem, vmem_limit_bytes=V7X_VMEM_LIMIT_BYTES)


def _n_ctx():
    return BATCH * SEQ


def _n_tok():
    return BATCH * SEQ + DEC_BATCH * DEC_SEQ


def _group_of_block(i, tm):
    n_ctx_blocks = _n_ctx() // tm
    return jnp.where(i < n_ctx_blocks, 0, 1 + (i - n_ctx_blocks) // (DEC_SEQ // tm))


def _silu(x):
    return x * jax.nn.sigmoid(x)


def _ada_body(c_ref, w_ref, b_ref, o_ref):
    s = _silu(c_ref[...]).astype(BF16)
    o_ref[...] = jnp.dot(s, w_ref[...].astype(BF16), preferred_element_type=F32) + b_ref[...]


def ada_all(cvec8, w_ada, b_ada, tn=512):
    depth, d, n6 = w_ada.shape
    return pl.pallas_call(
        _ada_body,
        grid=(depth, n6 // tn),
        in_specs=[pl.BlockSpec((8, d), lambda l, j: (0, 0)),
                  pl.BlockSpec((None, d, tn), lambda l, j: (l, 0, j)),
                  pl.BlockSpec((None, 1, tn), lambda l, j: (l, 0, j))],
        out_specs=pl.BlockSpec((None, 8, tn), lambda l, j: (l, 0, j)),
        out_shape=jax.ShapeDtypeStruct((depth, 8, n6), F32),
        compiler_params=_params("arbitrary", "arbitrary"),
        name="ada",
    )(cvec8, w_ada, b_ada.reshape(depth, 1, n6))


def _modulated(x, gain, shift, scale):
    y = x * lax.rsqrt(jnp.mean(x * x, axis=-1, keepdims=True) + NORM_EPS) * gain
    return y * (1.0 + scale) + shift


def _modulate_body(x_ref, gain_ref, sh_ref, sc_ref, o_ref):
    o_ref[...] = _modulated(x_ref[...], gain_ref[...], sh_ref[...], sc_ref[...]).astype(o_ref.dtype)


def _group_spec(d, tm):
    return pl.BlockSpec((None, 1, d), lambda i: (_group_of_block(i, tm), 0, 0))


def modulate(x, gain, shift, scale):
    n, d = x.shape
    tm = ROW_BLOCK
    return pl.pallas_call(
        _modulate_body,
        grid=(n // tm,),
        in_specs=[pl.BlockSpec((tm, d), lambda i: (i, 0)),
                  pl.BlockSpec((1, d), lambda i: (0, 0)),
                  _group_spec(d, tm), _group_spec(d, tm)],
        out_specs=pl.BlockSpec((tm, d), lambda i: (i, 0)),
        out_shape=jax.ShapeDtypeStruct((n, d), BF16),
        compiler_params=_params("arbitrary"),
        name="modulate",
    )(x, gain.reshape(1, d), shift, scale)


def _split_bf16(x):
    hi = x.astype(BF16)
    lo = (x - hi.astype(F32)).astype(BF16)
    return hi, lo


def _bf16_bits(x):
    return lax.bitcast_convert_type(x.astype(BF16).astype(F32), jnp.int32)


_UPPER_HALF = -65536


def _packed_row_words(d):
    return d // 2 + d // 4


def _router_body(x_ref, gain_ref, sh_ref, sc_ref, wr_ref, o_ref, p_ref, s_ref):
    h = _modulated(x_ref[...], gain_ref[...], sh_ref[...], sc_ref[...])
    o_ref[...] = h.astype(BF16)
    half = h.shape[1] // 2
    packed = lax.shift_right_logical(_bf16_bits(h[:, :half]), 16) | (_bf16_bits(h[:, half:]) & _UPPER_HALF)
    is_pad_block = pl.program_id(0) == pl.num_programs(0) - 1
    p_ref[:, :half] = jnp.where(is_pad_block, 0, packed)
    p_ref[:, half:] = jnp.zeros((h.shape[0], p_ref.shape[1] - half), jnp.int32)
    h_hi, h_lo = _split_bf16(h)
    w_hi, w_lo = _split_bf16(wr_ref[...])
    logits = (jnp.dot(h_hi, w_hi, preferred_element_type=F32)
              + jnp.dot(h_lo, w_hi, preferred_element_type=F32)
              + jnp.dot(h_hi, w_lo, preferred_element_type=F32))
    s_ref[...] = jax.nn.sigmoid(logits)


def modulate_and_route(x, gain, shift, scale, w_router):
    n, d = x.shape
    e = w_router.shape[1]
    tm = ROW_BLOCK
    n_rb = n // tm
    rb = lambda i: jnp.minimum(i, n_rb - 1)
    grp = pl.BlockSpec((None, 1, d), lambda i: (_group_of_block(rb(i), tm), 0, 0))
    return pl.pallas_call(
        _router_body,
        grid=(n_rb + 1,),
        in_specs=[pl.BlockSpec((tm, d), lambda i: (rb(i), 0)),
                  pl.BlockSpec((1, d), lambda i: (0, 0)),
                  grp, grp,
                  pl.BlockSpec((d, e), lambda i: (0, 0))],
        out_specs=[pl.BlockSpec((tm, d), lambda i: (rb(i), 0)),
                   pl.BlockSpec((tm, _packed_row_words(d)), lambda i: (i, 0)),
                   pl.BlockSpec((tm, e), lambda i: (rb(i), 0))],
        out_shape=[jax.ShapeDtypeStruct((n, d), BF16),
                   jax.ShapeDtypeStruct((n + tm, _packed_row_words(d)), jnp.int32),
                   jax.ShapeDtypeStruct((n, e), F32)],
        compiler_params=_params("arbitrary"),
        name="modulate_route",
    )(x, gain.reshape(1, d), shift, scale, w_router)


def _mm_body(*refs, n_w, n_extra, epilogue):
    a_ref = refs[0]
    w_refs = refs[1:1 + n_w]
    extra = refs[1 + n_w:1 + n_w + n_extra]
    o_ref = refs[1 + n_w + n_extra]
    w_bf = refs[2 + n_w + n_extra:]

    @pl.when(pl.program_id(1) == 0)
    def _():
        for w_ref, w_s in zip(w_refs, w_bf):
            w_s[...] = w_ref[...].astype(BF16)

    a = a_ref[...]
    accs = [jnp.dot(a, w_s[...], preferred_element_type=F32) for w_s in w_bf]
    o_ref[...] = epilogue(*accs, *[r[...] for r in extra]).astype(o_ref.dtype)


def _stacked(w, *lead):
    flat = 0
    for size, idx in zip(w.shape[:len(lead)], lead):
        flat = flat * size + idx
    return w.reshape((-1,) + w.shape[len(lead):]), flat


def matmul(name, a, ws, epilogue=None, extra=(), out_dtype=F32, tm=None, tn=None):
    m, k = a.shape
    n = ws[0][0].shape[2]
    tm = min(tm or MM_TM, m)
    tn = min(tn or MM_TN, n)
    assert m % tm == 0 and n % tn == 0, (name, m, n, tm, tn)
    if epilogue is None:
        epilogue = lambda acc: acc
    body = functools.partial(_mm_body, n_w=len(ws), n_extra=len(extra), epilogue=epilogue)

    def w_spec(idx):
        return pl.BlockSpec((None, k, tn), lambda j, i: (idx, 0, j))

    return pl.pallas_call(
        body,
        grid=(n // tn, m // tm),
        in_specs=([pl.BlockSpec((tm, k), lambda j, i: (i, 0))]
                  + [w_spec(idx) for _, idx in ws]
                  + [spec for _, spec in extra]),
        out_specs=pl.BlockSpec((tm, tn), lambda j, i: (i, j)),
        out_shape=jax.ShapeDtypeStruct((m, n), out_dtype),
        scratch_shapes=[pltpu.VMEM((k, tn), BF16) for _ in ws],
        compiler_params=_params("arbitrary", "arbitrary"),
        name=name,
    )(a, *[w for w, _ in ws], *[arr for arr, _ in extra])


def _row_vec_spec(tn):
    return pl.BlockSpec((1, tn), lambda j, i: (0, j))


def _gate_spec(tm, tn):
    return pl.BlockSpec((None, 1, tn), lambda j, i: (_group_of_block(i, tm), 0, j))


def _tile_spec(tm, tn):
    return pl.BlockSpec((tm, tn), lambda j, i: (i, j))


def _rms_head(x, gain):
    return x * lax.rsqrt(jnp.mean(x * x, axis=-1, keepdims=True) + NORM_EPS) * gain


def _rope(x, cos_f, sin_f):
    return x * cos_f + pltpu.roll(x, HEAD_DIM // 2, 1) * sin_f


def _attn_ctx_body(q_ref, k_ref, v_ref, qg_ref, kg_ref, o_ref, kc_ref, vc_ref):
    kn = _rms_head(k_ref[...], kg_ref[...])
    kc_ref[...] = kn
    v = v_ref[...]
    vc_ref[...] = v
    kb = kn.astype(BF16)
    vb = v.astype(BF16)
    scale = HEAD_DIM ** -0.5
    for h in range(N_HEADS // N_KV_HEADS):
        sl = slice(h * HEAD_DIM, (h + 1) * HEAD_DIM)
        qh = _rms_head(q_ref[:, sl], qg_ref[...]).astype(BF16)
        s = lax.dot_general(qh, kb, (((1,), (1,)), ((), ())), preferred_element_type=F32) * scale
        p = jnp.exp(s - jnp.max(s, axis=-1, keepdims=True))
        l = jnp.sum(p, axis=-1, keepdims=True)
        o = jnp.dot(p.astype(BF16), vb, preferred_element_type=F32) / l
        o_ref[:, sl] = o.astype(o_ref.dtype)


def attention_context(qkv, q_gain, k_gain):
    n = _n_ctx()
    g = N_HEADS // N_KV_HEADS
    hd = HEAD_DIM
    return pl.pallas_call(
        _attn_ctx_body,
        grid=(BATCH, N_KV_HEADS),
        in_specs=[pl.BlockSpec((SEQ, g * hd), lambda b, kh: (b, kh)),
                  pl.BlockSpec((SEQ, hd), lambda b, kh: (b, N_HEADS + kh)),
                  pl.BlockSpec((SEQ, hd), lambda b, kh: (b, N_HEADS + N_KV_HEADS + kh)),
                  pl.BlockSpec((1, hd), lambda b, kh: (0, 0)),
                  pl.BlockSpec((1, hd), lambda b, kh: (0, 0))],
        out_specs=[pl.BlockSpec((SEQ, g * hd), lambda b, kh: (b, kh)),
                   pl.BlockSpec((SEQ, hd), lambda b, kh: (b, kh)),
                   pl.BlockSpec((SEQ, hd), lambda b, kh: (b, kh))],
        out_shape=[jax.ShapeDtypeStruct((n, N_HEADS * hd), BF16),
                   jax.ShapeDtypeStruct((_n_ctx(), N_KV_HEADS * hd), F32),
                   jax.ShapeDtypeStruct((_n_ctx(), N_KV_HEADS * hd), F32)],
        compiler_params=_params("arbitrary", "arbitrary"),
        name="attn_ctx",
    )(qkv, qkv, qkv, q_gain.reshape(1, hd), k_gain.reshape(1, hd))


def _attn_lat_body(q_ref, k_ref, v_ref, ck_ref, cv_ref, cosq_ref, sinq_ref, cosk_ref, sink_ref,
                   qg_ref, kg_ref, o_ref):
    kn = _rope(_rms_head(k_ref[...], kg_ref[...]), cosk_ref[...], sink_ref[...]).astype(BF16)
    vb = v_ref[...].astype(BF16)
    ckb = ck_ref[...].astype(BF16)
    cvb = cv_ref[...].astype(BF16)
    scale = HEAD_DIM ** -0.5
    nt = (((1,), (1,)), ((), ()))
    for h in range(N_HEADS // N_KV_HEADS):
        sl = slice(h * HEAD_DIM, (h + 1) * HEAD_DIM)
        qh = _rope(_rms_head(q_ref[:, sl], qg_ref[...]), cosq_ref[...], sinq_ref[...]).astype(BF16)
        s_c = lax.dot_general(qh, ckb, nt, preferred_element_type=F32) * scale
        s_l = lax.dot_general(qh, kn, nt, preferred_element_type=F32) * scale
        m = jnp.maximum(jnp.max(s_c, axis=-1, keepdims=True), jnp.max(s_l, axis=-1, keepdims=True))
        p_c = jnp.exp(s_c - m)
        p_l = jnp.exp(s_l - m)
        l = jnp.sum(p_c, axis=-1, keepdims=True) + jnp.sum(p_l, axis=-1, keepdims=True)
        o = (jnp.dot(p_c.astype(BF16), cvb, preferred_element_type=F32)
             + jnp.dot(p_l.astype(BF16), vb, preferred_element_type=F32)) / l
        o_ref[:, sl] = o.astype(o_ref.dtype)


def attention_latent(qkv, cache_k, cache_v, cos_f, sin_f, q_gain, k_gain):
    g = N_HEADS // N_KV_HEADS
    hd = HEAD_DIM
    tq = min(ATTN_TQ, DEC_SEQ)
    nq = DEC_SEQ // tq
    q_row0 = _n_ctx() // tq
    k_row0 = _n_ctx() // DEC_SEQ
    q_map = lambda b, kh, qi: (q_row0 + b * nq + qi, kh)
    return pl.pallas_call(
        _attn_lat_body,
        grid=(DEC_BATCH, N_KV_HEADS, nq),
        in_specs=[pl.BlockSpec((tq, g * hd), q_map),
                  pl.BlockSpec((DEC_SEQ, hd), lambda b, kh, qi: (k_row0 + b, N_HEADS + kh)),
                  pl.BlockSpec((DEC_SEQ, hd), lambda b, kh, qi: (k_row0 + b, N_HEADS + N_KV_HEADS + kh)),
                  pl.BlockSpec((None, None, PAST_LEN, hd), lambda b, kh, qi: (b, kh, 0, 0)),
                  pl.BlockSpec((None, None, PAST_LEN, hd), lambda b, kh, qi: (b, kh, 0, 0)),
                  pl.BlockSpec((tq, hd), lambda b, kh, qi: (qi, 0)),
                  pl.BlockSpec((tq, hd), lambda b, kh, qi: (qi, 0)),
                  pl.BlockSpec((DEC_SEQ, hd), lambda b, kh, qi: (0, 0)),
                  pl.BlockSpec((DEC_SEQ, hd), lambda b, kh, qi: (0, 0)),
                  pl.BlockSpec((1, hd), lambda b, kh, qi: (0, 0)),
                  pl.BlockSpec((1, hd), lambda b, kh, qi: (0, 0))],
        out_specs=pl.BlockSpec((tq, g * hd), lambda b, kh, qi: (b * nq + qi, kh)),
        out_shape=jax.ShapeDtypeStruct((DEC_BATCH * DEC_SEQ, N_HEADS * hd), BF16),
        compiler_params=_params("arbitrary", "arbitrary", "arbitrary"),
        name="attn_lat",
    )(qkv, qkv, qkv, cache_k, cache_v, cos_f, sin_f, cos_f, sin_f,
      q_gain.reshape(1, hd), k_gain.reshape(1, hd))


def rope_tables():
    rows = DEC_SEQ // GRID_W
    r_idx, c_idx = jnp.meshgrid(jnp.arange(rows), jnp.arange(GRID_W), indexing="ij")
    r_idx = r_idx.reshape(-1).astype(F32)
    c_idx = c_idx.reshape(-1).astype(F32)
    n_freq = HEAD_DIM // 4
    inv = ROPE_THETA ** (-jnp.arange(n_freq, dtype=F32) / n_freq)
    ang = jnp.concatenate([r_idx[:, None] * inv, c_idx[:, None] * inv], axis=-1)
    cos, sin = jnp.cos(ang), jnp.sin(ang)
    return jnp.concatenate([cos, cos], axis=-1), jnp.concatenate([-sin, sin], axis=-1)


def _mix_body(x_ref, xp_ref, xn_ref, gain_ref, sh_ref, sc_ref, mu_ref, *o_refs):
    i = pl.program_id(0)
    tm = x_ref.shape[0]
    n_ctx_blocks = _n_ctx() // tm
    per_seq = DEC_SEQ // tm
    j = (i - n_ctx_blocks) % per_seq
    is_first = jnp.logical_or(i < n_ctx_blocks, j == 0)
    is_last = jnp.logical_or(i < n_ctx_blocks, j == per_seq - 1)
    gain, sh, sc = gain_ref[...], sh_ref[...], sc_ref[...]
    h = _modulated(x_ref[...], gain, sh, sc)
    h_prev = jnp.where(is_first, 0.0, _modulated(xp_ref[7:8, :], gain, sh, sc))
    h_next = jnp.where(is_last, 0.0, _modulated(xn_ref[0:1, :], gain, sh, sc))
    row = lax.broadcasted_iota(jnp.int32, h.shape, 0)
    prev = jnp.where(row == 0, h_prev, pltpu.roll(h, 1, 0))
    nxt = jnp.where(row == tm - 1, h_next, pltpu.roll(h, tm - 1, 0))
    xx = 0.5 * (prev + nxt) - h
    for n, o_ref in enumerate(o_refs):
        o_ref[...] = (h + xx * mu_ref[n:n + 1, :]).astype(o_ref.dtype)


def rwkv_mix(x, gain, shift, scale, mu):
    n, d = x.shape
    tm = ROW_BLOCK
    rb = tm // 8
    last8 = n // 8 - 1
    return pl.pallas_call(
        _mix_body,
        grid=(n // tm,),
        in_specs=[pl.BlockSpec((tm, d), lambda i: (i, 0)),
                  pl.BlockSpec((8, d), lambda i: (jnp.maximum(i * rb - 1, 0), 0)),
                  pl.BlockSpec((8, d), lambda i: (jnp.minimum((i + 1) * rb, last8), 0)),
                  pl.BlockSpec((1, d), lambda i: (0, 0)),
                  _group_spec(d, tm), _group_spec(d, tm),
                  pl.BlockSpec((mu.shape[0], d), lambda i: (0, 0))],
        out_specs=[pl.BlockSpec((tm, d), lambda i: (i, 0)) for _ in range(6)],
        out_shape=[jax.ShapeDtypeStruct((n, d), BF16) for _ in range(6)],
        compiler_params=_params("arbitrary"),
        name="rwkv_mix",
    )(x, x, x, gain.reshape(1, d), shift, scale, mu)


def _seg_sum(x, ones_bd):
    hi, lo = _split_bf16(x)
    return (jnp.dot(hi, ones_bd, preferred_element_type=F32)
            + jnp.dot(lo, ones_bd, preferred_element_type=F32))


def _scan_body(r_ref, k_ref, v_ref, wl_ref, al_ref, g_ref, kk_ref, ka_ref, rk_ref, lnw_ref, lnb_ref,
               s0_ref, *rest, reverse, add_prev):
    if add_prev:
        prev_ref, o_ref, sfin_ref, s_scr = rest
    else:
        o_ref, sfin_ref, s_scr = rest
    L = SCAN_CHUNK
    R = 2 * L
    hw = RWKV_HEAD
    n_pairs = s_scr.shape[0]
    n_il = min(SCAN_INTERLEAVE, n_pairs)
    step = pl.program_id(0)
    n_chunks = pl.num_programs(0)
    c = (n_chunks - 1 - step) if reverse else step
    ctx_chunks = _n_ctx() // L
    per_ctx, per_lat = SEQ // L, DEC_SEQ // L
    pos = jnp.where(c < ctx_chunks, c % per_ctx, (c - ctx_chunks) % per_lat)
    seq_len = jnp.where(c < ctx_chunks, per_ctx, per_lat)
    first_pos = seq_len - 1 if reverse else 0
    last_pos = 0 if reverse else seq_len - 1

    @pl.when(pos == first_pos)
    def _():
        s_scr[...] = s0_ref[...]

    ri = lax.broadcasted_iota(jnp.int32, (L, L), 0)
    ci = lax.broadcasted_iota(jnp.int32, (L, L), 1)
    before = (ci > ri) if reverse else (ci < ri)
    incl_bf = jnp.logical_or(before, ci == ri).astype(BF16)
    r4 = lax.broadcasted_iota(jnp.int32, (2 * R, 2 * R), 0)
    c4 = lax.broadcasted_iota(jnp.int32, (2 * R, 2 * R), 1)
    same_head = ((r4 % R) // L) == ((c4 % R) // L)
    t4, j4 = r4 % L, c4 % L
    before4 = (j4 > t4) if reverse else (j4 < t4)
    mask4 = jnp.logical_and(same_head, jnp.logical_or(before4, jnp.logical_and(r4 >= R, j4 == t4)))
    eye = (lax.broadcasted_iota(jnp.int32, (R, R), 0) == lax.broadcasted_iota(jnp.int32, (R, R), 1)).astype(F32)
    lr = lax.broadcasted_iota(jnp.int32, (LANES, LANES), 0)
    lc = lax.broadcasted_iota(jnp.int32, (LANES, LANES), 1)
    ones_bd = ((lr // hw) == (lc // hw)).astype(BF16)
    head_a = lax.broadcasted_iota(jnp.int32, (L, LANES), 1) < hw
    nt = (((1,), (1,)), ((), ()))
    inv_hw = 1.0 / hw
    Q = range(n_il)

    def stack(x):
        return jnp.concatenate([jnp.where(head_a, x, 0.0), jnp.where(head_a, 0.0, x)], axis=0)

    def mm(x, y):
        return jnp.dot(x.astype(BF16), y.astype(BF16), preferred_element_type=F32)

    def group(gi, carry):
        sls = [pl.ds(pl.multiple_of((gi * n_il + q) * LANES, LANES), LANES) for q in Q]
        s_vk = [s_scr[gi * n_il + q] for q in Q]
        r = [r_ref[:, sl] for sl in sls]
        k = [k_ref[:, sl] for sl in sls]
        v = [v_ref[:, sl] for sl in sls]
        lw = [DECAY_SCALE * jax.nn.sigmoid(wl_ref[:, sl]) for sl in sls]
        ag = [jax.nn.sigmoid(al_ref[:, sl]) for sl in sls]
        kkf = [k[q] * kk_ref[:, sls[q]] for q in Q]
        nrm = [_seg_sum(kkf[q] * kkf[q], ones_bd) for q in Q]
        kk = [kkf[q] / jnp.maximum(jnp.sqrt(nrm[q]), 1e-12) for q in Q]
        kd = [k[q] * (1.0 + (ag[q] - 1.0) * ka_ref[:, sls[q]]) for q in Q]
        cs = []
        for q in Q:
            w_hi = lw[q].astype(BF16)
            w_r1 = lw[q] - w_hi.astype(F32)
            w_mid = w_r1.astype(BF16)
            w_lo = (w_r1 - w_mid.astype(F32)).astype(BF16)
            cs.append(jnp.dot(incl_bf, w_hi, preferred_element_type=F32)
                      + jnp.dot(incl_bf, w_mid, preferred_element_type=F32)
                      + jnp.dot(incl_bf, w_lo, preferred_element_type=F32))
        e_neg = [jnp.exp(-cs[q]) for q in Q]
        ar = [jnp.concatenate([stack(jnp.exp(cs[q] - lw[q]) * -kk[q]), stack(jnp.exp(cs[q]) * r[q])],
                              axis=0).astype(BF16) for q in Q]
        bk = [jnp.concatenate([stack(e_neg[q] * kk[q] * ag[q]), stack(e_neg[q] * kd[q])],
                              axis=0).astype(BF16) for q in Q]
        v_s = [stack(v[q]).astype(BF16) for q in Q]
        gm = [jnp.where(mask4, lax.dot_general(ar[q], bk[q], nt, preferred_element_type=F32), 0.0) for q in Q]
        a_s = [lax.dot_general(ar[q], s_vk[q].astype(BF16), nt, preferred_element_type=F32) for q in Q]
        w0 = [a_s[q][:R] + mm(gm[q][:R, R:], v_s[q]) for q in Q]
        m = [eye + gm[q][:R, :R] for q in Q]
        pw = [gm[q][:R, :R] for q in Q]
        n = 1
        while n < L // 2:
            pw = [mm(pw[q], pw[q]) for q in Q]
            m = [m[q] + mm(m[q], pw[q]) for q in Q]
            n *= 2
        u_s = [mm(m[q], w0[q]) for q in Q]
        uv = [jnp.concatenate([u_s[q], v_s[q].astype(F32)], axis=0) for q in Q]
        y_s = [a_s[q][R:] + mm(gm[q][R:, :], uv[q]) for q in Q]
        y = [y_s[q][:L] + y_s[q][L:] for q in Q]
        upd = [mm(uv[q].T, bk[q]) for q in Q]
        for q in Q:
            c_tot = cs[q][0:1, :] if reverse else cs[q][L - 1:L, :]
            s_scr[gi * n_il + q] = (s_vk[q] + upd[q]) * jnp.exp(c_tot)
        mean = [_seg_sum(y[q], ones_bd) * inv_hw for q in Q]
        dy = [y[q] - mean[q] for q in Q]
        var = [_seg_sum(dy[q] * dy[q], ones_bd) * inv_hw for q in Q]
        bonus = [_seg_sum(r[q] * kd[q] * rk_ref[:, sls[q]], ones_bd) * v[q] for q in Q]
        for q in Q:
            sl = sls[q]
            yn = dy[q] * lax.rsqrt(var[q] + GN_EPS) * lnw_ref[:, sl] + lnb_ref[:, sl]
            out = (yn + bonus[q]) * g_ref[:, sl]
            if add_prev:
                out = out + prev_ref[:, sl].astype(F32)
            o_ref[:, sl] = out.astype(o_ref.dtype)
        return carry

    lax.fori_loop(0, n_pairs // n_il, group, 0)

    @pl.when(pos == last_pos)
    def _():
        sfin_ref[...] = s_scr[...]


def wkv_scan_dir(r, k, v, wl, al, g, k_k, k_a, r_k, lnx_w, lnx_b, s0, reverse, prev=None):
    n, d = r.shape
    L = SCAN_CHUNK
    n_chunks = n // L
    n_pairs = d // LANES
    ctx_chunks = _n_ctx() // L

    def cidx(s):
        return (n_chunks - 1 - s) if reverse else s

    def seq_of(s):
        c = cidx(s)
        return jnp.where(c < ctx_chunks, c // (SEQ // L), BATCH + (c - ctx_chunks) // (DEC_SEQ // L))

    tok = pl.BlockSpec((L, d), lambda s: (cidx(s), 0))
    vec = pl.BlockSpec((1, d), lambda s: (0, 0))
    st = pl.BlockSpec((None, n_pairs, LANES, LANES), lambda s: (seq_of(s), 0, 0, 0))
    add_prev = prev is not None
    body = functools.partial(_scan_body, reverse=reverse, add_prev=add_prev)
    args = [r, k, v, wl, al, g, k_k.reshape(1, d), k_a.reshape(1, d), r_k.reshape(1, d),
            lnx_w.reshape(1, d), lnx_b.reshape(1, d), s0]
    in_specs = [tok] * 6 + [vec] * 5 + [st]
    if add_prev:
        args.append(prev)
        in_specs.append(tok)
    return pl.pallas_call(
        body,
        grid=(n_chunks,),
        in_specs=in_specs,
        out_specs=[tok, st],
        out_shape=[jax.ShapeDtypeStruct((n, d), BF16 if add_prev else F32),
                   jax.ShapeDtypeStruct(s0.shape, F32)],
        scratch_shapes=[pltpu.VMEM((n_pairs, LANES, LANES), F32)],
        compiler_params=_params("arbitrary"),
        name="wkv_bwd" if reverse else "wkv_fwd",
    )(*args)


def _pair_states(s):
    n_seq, h, hv, hk = s.shape
    s = s.reshape(n_seq, h // 2, 2, hv, hk)
    z = jnp.zeros_like(s[:, :, 0])
    top = jnp.concatenate([s[:, :, 0], z], axis=-1)
    bot = jnp.concatenate([z, s[:, :, 1]], axis=-1)
    return jnp.concatenate([top, bot], axis=-2)


def _unpair_states(s):
    n_seq, n_pairs = s.shape[:2]
    hw = RWKV_HEAD
    a = s[:, :, :hw, :hw]
    b = s[:, :, hw:, hw:]
    return jnp.stack([a, b], axis=2).reshape(n_seq, 2 * n_pairs, hw, hw)


ITEM_BLOCKS = 4


def _store_rows(o_ref, value):
    rows = value.shape[0]
    o_ref[:rows, :] = value.astype(o_ref.dtype)
    if rows < o_ref.shape[0]:
        o_ref[rows:, :] = jnp.zeros((o_ref.shape[0] - rows, o_ref.shape[1]), o_ref.dtype)


def _expert_up_body(ie_ref, ib_ref, inb_ref, last_ref, *refs):
    del ie_ref, ib_ref, last_ref
    x_refs = refs[:ITEM_BLOCKS]
    wg_ref, wu_ref, o_ref = refs[ITEM_BLOCKS:]
    nb = inb_ref[pl.program_id(0)]
    half = x_refs[0].shape[1]
    for n_sub in range(1, ITEM_BLOCKS + 1):
        @pl.when(nb == n_sub)
        def _(n_sub=n_sub):
            xp = jnp.concatenate([x_refs[s][...] for s in range(n_sub)], axis=0)
            x_lo = lax.bitcast_convert_type(xp << 16, F32).astype(BF16)
            x_hi = lax.bitcast_convert_type(xp & _UPPER_HALF, F32).astype(BF16)
            wg = wg_ref[...].astype(BF16)
            wu = wu_ref[...].astype(BF16)
            gate = (jnp.dot(x_lo, wg[:half], preferred_element_type=F32)
                    + jnp.dot(x_hi, wg[half:], preferred_element_type=F32))
            up = (jnp.dot(x_lo, wu[:half], preferred_element_type=F32)
                  + jnp.dot(x_hi, wu[half:], preferred_element_type=F32))
            _store_rows(o_ref, _silu(gate) * up)


def _item_x_spec(s, bm, half):
    def index(it, f, ie, ib, inb, last):
        return ib[it] + jnp.minimum(s, jnp.maximum(inb[it] - 1, 0)), 0
    return pl.BlockSpec((bm, half), index)


def expert_up(xs, w_gate, w_up, layer, item_e, item_blk, item_nb, last_item, tf=256):
    d = w_gate.shape[2]
    half = d // 2
    n_exp = w_gate.shape[1]
    ff = w_gate.shape[-1]
    w_gate = w_gate.reshape((-1,) + w_gate.shape[2:])
    w_up = w_up.reshape((-1,) + w_up.shape[2:])
    bm = ROW_BLOCK
    tf = min(tf, ff)
    n_f = ff // tf
    n_items = item_e.shape[0]
    item_rows = ITEM_BLOCKS * bm

    def col(it, f, last):
        return jnp.where(it > last[0], n_f - 1, f)

    w_spec = pl.BlockSpec((None, d, tf),
                          lambda it, f, ie, ib, inb, last: (layer * n_exp + ie[it], 0, col(it, f, last)))
    return pl.pallas_call(
        _expert_up_body,
        grid_spec=pltpu.PrefetchScalarGridSpec(
            num_scalar_prefetch=4,
            grid=(n_items, n_f),
            in_specs=[_item_x_spec(s, bm, half) for s in range(ITEM_BLOCKS)] + [w_spec, w_spec],
            out_specs=pl.BlockSpec((item_rows, tf),
                                   lambda it, f, ie, ib, inb, last: (jnp.minimum(it, last[0]), col(it, f, last)))),
        out_shape=jax.ShapeDtypeStruct((n_items * item_rows, ff), BF16),
        compiler_params=_params("arbitrary", "arbitrary"),
        name="expert_up",
    )(item_e, item_blk, item_nb, last_item, *([xs] * ITEM_BLOCKS), w_gate, w_up)


def _expert_down_body(ie_ref, ib_ref, inb_ref, last_ref, h_ref, wd_ref, o_ref):
    del ie_ref, ib_ref, last_ref
    nb = inb_ref[pl.program_id(1)]
    bm = h_ref.shape[0] // ITEM_BLOCKS
    for n_sub in range(1, ITEM_BLOCKS + 1):
        @pl.when(nb == n_sub)
        def _(n_sub=n_sub):
            y = jnp.dot(h_ref[:n_sub * bm, :], wd_ref[...].astype(BF16), preferred_element_type=F32)
            _store_rows(o_ref, y)


def expert_down(hm, w_down, layer, item_e, item_blk, item_nb, last_item, tn=2048):
    n_rows, ff = hm.shape
    d = w_down.shape[-1]
    n_exp = w_down.shape[1]
    w_down = w_down.reshape((-1,) + w_down.shape[2:])
    tn = min(tn, d)
    n_items = item_e.shape[0]
    item_rows = n_rows // n_items
    return pl.pallas_call(
        _expert_down_body,
        grid_spec=pltpu.PrefetchScalarGridSpec(
            num_scalar_prefetch=4,
            grid=(d // tn, n_items),
            in_specs=[pl.BlockSpec((item_rows, ff), lambda j, it, ie, ib, inb, last: (jnp.minimum(it, last[0]), 0)),
                      pl.BlockSpec((None, ff, tn), lambda j, it, ie, ib, inb, last: (layer * n_exp + ie[it], 0, j))],
            out_specs=pl.BlockSpec((item_rows, tn), lambda j, it, ie, ib, inb, last: (jnp.minimum(it, last[0]), j))),
        out_shape=jax.ShapeDtypeStruct((n_rows, d), F32),
        compiler_params=_params("arbitrary", "arbitrary"),
        name="expert_down",
    )(item_e, item_blk, item_nb, last_item, hm, w_down)


def route(scores, b_router):
    n = scores.shape[0]
    bm = ROW_BLOCK
    _, idx = lax.top_k(scores + b_router.astype(F32), TOP_K)
    sel = jnp.take_along_axis(scores, idx, axis=-1)
    gw = sel / jnp.sum(sel, axis=-1, keepdims=True) * ROUTED_SCALE
    a = n * TOP_K
    n_blocks = a // bm + N_EXPERTS
    e_flat = idx.reshape(-1)
    onehot = (e_flat[:, None] == jnp.arange(N_EXPERTS)[None, :]).astype(jnp.int32)
    rank = jnp.take_along_axis(jnp.cumsum(onehot, axis=0), e_flat[:, None], axis=1)[:, 0] - 1
    counts = jnp.sum(onehot, axis=0)
    padded = (counts + bm - 1) // bm * bm
    pad_end = jnp.cumsum(padded)
    pad_start = pad_end - padded
    dest = (pad_start[e_flat] + rank).astype(jnp.int32)
    tok_flat = jnp.arange(a, dtype=jnp.int32) // TOP_K
    slot_tok = jnp.full((n_blocks * bm,), n, jnp.int32).at[dest].set(tok_flat)
    ib = ITEM_BLOCKS
    n_items = n_blocks // ib + N_EXPERTS
    e_blocks = padded // bm
    e_items = (e_blocks + ib - 1) // ib
    item_end = jnp.cumsum(e_items)
    item_begin = item_end - e_items
    last_item = (item_end[-1:] - 1).astype(jnp.int32)
    it = jnp.minimum(jnp.arange(n_items), last_item[0])
    item_e = jnp.minimum(jnp.searchsorted(item_end, it, side="right"), N_EXPERTS - 1).astype(jnp.int32)
    local = it - item_begin[item_e]
    item_blk = (pad_start[item_e] // bm + ib * local).astype(jnp.int32)
    item_nb = jnp.where(jnp.arange(n_items) <= last_item[0],
                        jnp.clip(e_blocks[item_e] - ib * local, 0, ib), 0).astype(jnp.int32)
    item_rows = ib * bm
    out_row = ((item_begin[e_flat] + rank // item_rows) * item_rows + rank % item_rows).astype(jnp.int32)
    return gw, out_row.reshape(n, TOP_K), slot_tok, item_e, item_blk, item_nb, last_item


def kernel(x_prompt, x_sample, c, cache_k, cache_v, state_wkv, c_ctx, w_ada, b_ada, norm_mix, norm_ffn, attn_w_qkv, attn_w_o, attn_q_gain, attn_k_gain, rw_mu, rw_w_r, rw_w_k, rw_w_v, rw_w_o, rw_w0, rw_w1, rw_w2, rw_a0, rw_a1, rw_a2, rw_g1, rw_g2, rw_k_k, rw_k_a, rw_r_k, rw_lnx_w, rw_lnx_b, moe_w_router, moe_b_router, moe_w_gate, moe_w_up, moe_w_down, sh_w_gate, sh_w_up, sh_w_down):
    d = D_MODEL
    n_ctx, n = _n_ctx(), _n_tok()
    n_grp = 1 + DEC_BATCH
    x = jnp.concatenate([x_prompt.reshape(n_ctx, d), x_sample.reshape(n - n_ctx, d)], axis=0)

    cvec8 = jnp.zeros((8, d), F32).at[0].set(c_ctx).at[1:n_grp].set(c)
    mods = ada_all(cvec8, w_ada, b_ada)

    def mod(layer, m):
        return mods[layer, :n_grp, m * d:(m + 1) * d].reshape(n_grp, 1, d)

    tm, tn = min(MM_TM, n), min(MM_TN, d)
    resid_gate = lambda acc, res, gate: res + gate * acc

    new_k = new_v = new_s = None
    for i in range(DEPTH):
        j = i // 2
        sh1, sc1, g1, sh2, sc2, g2 = [mod(i, m) for m in range(N_MOD)]
        if i % 2 == 0:
            h = modulate(x, norm_mix[i], sh1, sc1)
            qkv = matmul("qkv", h, [_stacked(attn_w_qkv, j)])
            o_ctx, kc, vc = attention_context(qkv, attn_q_gain[j], attn_k_gain[j])
            cos_f, sin_f = rope_tables()
            o_lat = attention_latent(qkv, jnp.swapaxes(cache_k[:, j], 1, 2), jnp.swapaxes(cache_v[:, j], 1, 2),
                                     cos_f, sin_f, attn_q_gain[j], attn_k_gain[j])
            o = jnp.concatenate([o_ctx, o_lat], axis=0)
            new_k = kc.reshape(BATCH, 1, SEQ, N_KV_HEADS, HEAD_DIM)
            new_v = vc.reshape(BATCH, 1, SEQ, N_KV_HEADS, HEAD_DIM)
            w_o = _stacked(attn_w_o, j)
        else:
            xr, xw, xk, xv, xa, xg = rwkv_mix(x, norm_mix[i], sh1, sc1, rw_mu[j])
            r = matmul("rwkv_r", xr, [_stacked(rw_w_r, j)])
            k = matmul("rwkv_k", xk, [_stacked(rw_w_k, j)])
            v = matmul("rwkv_v", xv, [_stacked(rw_w_v, j)])
            lw = DECAY_LORA
            la = AAA_LORA
            lg = GATE_LORA
            lgp = -(-lg // LANES) * LANES
            w_mid = matmul("decay_lora_in", xw, [(jnp.concatenate([rw_w1[j, 0], rw_w1[j, 1]], axis=1)[None], 0)],
                           epilogue=jnp.tanh, out_dtype=BF16)
            a_mid = matmul("rate_lora_in", xa, [(jnp.concatenate([rw_a1[j, 0], rw_a1[j, 1]], axis=1)[None], 0)],
                           out_dtype=BF16)
            g1p = jnp.pad(rw_g1[j], ((0, 0), (0, 0), (0, lgp - lg)))
            g2p = jnp.pad(rw_g2[j], ((0, 0), (0, lgp - lg), (0, 0)))
            g_mid = matmul("gate_lora_in", xg, [(jnp.concatenate([g1p[0], g1p[1]], axis=1)[None], 0)],
                           epilogue=jax.nn.sigmoid, out_dtype=BF16)
            add_vec = lambda acc, vec: acc + vec
            s0_all = jnp.concatenate(
                [jnp.zeros((BATCH, 2) + state_wkv.shape[3:], F32), state_wkv[:, j]], axis=0)
            o = None
            finals = []
            for dr in range(2):
                wl = matmul("decay_lora_out", w_mid[:, dr * lw:(dr + 1) * lw], [_stacked(rw_w2, j, dr)],
                            epilogue=add_vec, extra=[(rw_w0[j, dr].reshape(1, d), _row_vec_spec(tn))])
                al = matmul("rate_lora_out", a_mid[:, dr * la:(dr + 1) * la], [_stacked(rw_a2, j, dr)],
                            epilogue=add_vec, extra=[(rw_a0[j, dr].reshape(1, d), _row_vec_spec(tn))])
                gg = matmul("gate_lora_out", g_mid[:, dr * lgp:(dr + 1) * lgp], [(g2p, dr)])
                o, s_fin = wkv_scan_dir(r, k, v, wl, al, gg, rw_k_k[j], rw_k_a[j], rw_r_k[j, dr],
                                        rw_lnx_w[j], rw_lnx_b[j], _pair_states(s0_all[:, dr]),
                                        reverse=(dr == 1), prev=o)
                finals.append(_unpair_states(s_fin[:BATCH]))
            new_s = jnp.stack(finals, axis=1)[:, None]
            w_o = _stacked(rw_w_o, j)
        x = matmul("mixer_out", o, [w_o], epilogue=resid_gate,
                   extra=[(x, _tile_spec(tm, tn)), (g1, _gate_spec(tm, tn))])

        h2, h2_packed, scores = modulate_and_route(x, norm_ffn[i], sh2, sc2, moe_w_router[i])
        gw, out_row, slot_tok, *items = route(scores, moe_b_router[i])
        xs = jnp.take(h2_packed, slot_tok, axis=0, mode="clip")
        hs = matmul("shared_up", h2, [_stacked(sh_w_gate, i), _stacked(sh_w_up, i)],
                    epilogue=lambda a, b: _silu(a) * b, out_dtype=BF16, tn=256)
        xs, hs = lax.optimization_barrier((xs, hs))
        hm = expert_up(xs, moe_w_gate, moe_w_up, i, *items)
        yb = expert_down(hm, moe_w_down, i, *items)
        routed = jnp.sum(jnp.take(yb, out_row, axis=0, mode="clip") * gw[:, :, None], axis=1)
        moe_out = lambda acc, rt, res, gate: res + gate * (acc + rt)
        x = matmul("moe_out", hs, [_stacked(sh_w_down, i)], epilogue=moe_out,
                   extra=[(routed, _tile_spec(tm, tn)), (x, _tile_spec(tm, tn)), (g2, _gate_spec(tm, tn))])

    y_prompt = x[:n_ctx].reshape(BATCH, SEQ, d)
    y_sample = x[n_ctx:].reshape(DEC_BATCH, DEC_SEQ, d)
    return (y_prompt, y_sample, new_k, new_v, new_s)
```

```python
import functools

import jax
import jax.numpy as jnp
from jax import lax
from jax.experimental import pallas as pl
from jax.experimental.pallas import tpu as pltpu

D_MODEL = 4096
BATCH = 16
SEQ = 256
DEPTH = 2
DEC_BATCH = 2
DEC_SEQ = 1024
PAST_LEN = 512
GRID_W = 64
N_HEADS = 32
N_KV_HEADS = 8
HEAD_DIM = 128
ROPE_THETA = 10000.0
RWKV_HEAD = 64
DECAY_LORA = 128
AAA_LORA = 128
GATE_LORA = 480
N_EXPERTS = 64
TOP_K = 8
EXPERT_FF = 1024
SHARED_FF = 1024
ROUTED_SCALE = 2.5
N_MOD = 6
NORM_EPS = 1e-6
GN_EPS = 64e-5
DECAY_SCALE = -0.606531

F32 = jnp.float32
BF16 = jnp.bfloat16

V7X_VMEM_LIMIT_BYTES = 56 * 1024 * 1024
LANES = 128
ROW_BLOCK = 256
MM_TM = 1024
MM_TN = 512
SCAN_CHUNK = 64
SCAN_INTERLEAVE = 8
ATTN_TQ = 512


def _params(*sem):
    return pltpu.CompilerParams(dimension_semantics=sem, vmem_limit_bytes=V7X_VMEM_LIMIT_BYTES)


def _n_ctx():
    return BATCH * SEQ


def _n_tok():
    return BATCH * SEQ + DEC_BATCH * DEC_SEQ


def _group_of_block(i, tm):
    n_ctx_blocks = _n_ctx() // tm
    return jnp.where(i < n_ctx_blocks, 0, 1 + (i - n_ctx_blocks) // (DEC_SEQ // tm))


def _silu(x):
    return x * jax.nn.sigmoid(x)


def _ada_body(c_ref, w_ref, b_ref, o_ref):
    s = _silu(c_ref[...]).astype(BF16)
    o_ref[...] = jnp.dot(s, w_ref[...].astype(BF16), preferred_element_type=F32) + b_ref[...]


def ada_all(cvec8, w_ada, b_ada, tn=512):
    depth, d, n6 = w_ada.shape
    return pl.pallas_call(
        _ada_body,
        grid=(depth, n6 // tn),
        in_specs=[pl.BlockSpec((8, d), lambda l, j: (0, 0)),
                  pl.BlockSpec((None, d, tn), lambda l, j: (l, 0, j)),
                  pl.BlockSpec((None, 1, tn), lambda l, j: (l, 0, j))],
        out_specs=pl.BlockSpec((None, 8, tn), lambda l, j: (l, 0, j)),
        out_shape=jax.ShapeDtypeStruct((depth, 8, n6), F32),
        compiler_params=_params("arbitrary", "arbitrary"),
        name="ada",
    )(cvec8, w_ada, b_ada.reshape(depth, 1, n6))


def _modulated(x, gain, shift, scale):
    y = x * lax.rsqrt(jnp.mean(x * x, axis=-1, keepdims=True) + NORM_EPS) * gain
    return y * (1.0 + scale) + shift


def _modulate_body(x_ref, gain_ref, sh_ref, sc_ref, o_ref):
    o_ref[...] = _modulated(x_ref[...], gain_ref[...], sh_ref[...], sc_ref[...]).astype(o_ref.dtype)


def _group_spec(d, tm):
    return pl.BlockSpec((None, 1, d), lambda i: (_group_of_block(i, tm), 0, 0))


def modulate(x, gain, shift, scale):
    n, d = x.shape
    tm = ROW_BLOCK
    return pl.pallas_call(
        _modulate_body,
        grid=(n // tm,),
        in_specs=[pl.BlockSpec((tm, d), lambda i: (i, 0)),
                  pl.BlockSpec((1, d), lambda i: (0, 0)),
                  _group_spec(d, tm), _group_spec(d, tm)],
        out_specs=pl.BlockSpec((tm, d), lambda i: (i, 0)),
        out_shape=jax.ShapeDtypeStruct((n, d), BF16),
        compiler_params=_params("arbitrary"),
        name="modulate",
    )(x, gain.reshape(1, d), shift, scale)


def _split_bf16(x):
    hi = x.astype(BF16)
    lo = (x - hi.astype(F32)).astype(BF16)
    return hi, lo


def _bf16_bits(x):
    return lax.bitcast_convert_type(x.astype(BF16).astype(F32), jnp.int32)


_UPPER_HALF = -65536


def _pack_bf16_pairs(lo, hi):
    return lax.shift_right_logical(_bf16_bits(lo), 16) | (_bf16_bits(hi) & _UPPER_HALF)


def _unpack_bf16_pairs(words):
    return (lax.bitcast_convert_type(words << 16, F32),
            lax.bitcast_convert_type(words & _UPPER_HALF, F32))


def _packed_row_words(d):
    return d // 2 + d // 8


PACK_COLS = 1024
PACK_PAD = 128


def _pack_cols(d):
    return min(PACK_COLS, d)


def _pack_tile_words(d):
    return _pack_cols(d) // 2 + PACK_PAD


def _router_body(x_ref, gain_ref, sh_ref, sc_ref, wr_ref, o_ref, p_ref, s_ref):
    h = _modulated(x_ref[...], gain_ref[...], sh_ref[...], sc_ref[...])
    o_ref[...] = h.astype(BF16)
    half = h.shape[1] // 2
    packed = _pack_bf16_pairs(h[:, :half], h[:, half:])
    is_pad_block = pl.program_id(0) == pl.num_programs(0) - 1
    p_ref[:, :half] = jnp.where(is_pad_block, 0, packed)
    p_ref[:, half:] = jnp.zeros((h.shape[0], p_ref.shape[1] - half), jnp.int32)
    h_hi, h_lo = _split_bf16(h)
    w_hi, w_lo = _split_bf16(wr_ref[...])
    logits = (jnp.dot(h_hi, w_hi, preferred_element_type=F32)
              + jnp.dot(h_lo, w_hi, preferred_element_type=F32)
              + jnp.dot(h_hi, w_lo, preferred_element_type=F32))
    s_ref[...] = jax.nn.sigmoid(logits)


def modulate_and_route(x, gain, shift, scale, w_router):
    n, d = x.shape
    e = w_router.shape[1]
    tm = ROW_BLOCK
    n_rb = n // tm
    rb = lambda i: jnp.minimum(i, n_rb - 1)
    grp = pl.BlockSpec((None, 1, d), lambda i: (_group_of_block(rb(i), tm), 0, 0))
    return pl.pallas_call(
        _router_body,
        grid=(n_rb + 1,),
        in_specs=[pl.BlockSpec((tm, d), lambda i: (rb(i), 0)),
                  pl.BlockSpec((1, d), lambda i: (0, 0)),
                  grp, grp,
                  pl.BlockSpec((d, e), lambda i: (0, 0))],
        out_specs=[pl.BlockSpec((tm, d), lambda i: (rb(i), 0)),
                   pl.BlockSpec((tm, _packed_row_words(d)), lambda i: (i, 0)),
                   pl.BlockSpec((tm, e), lambda i: (rb(i), 0))],
        out_shape=[jax.ShapeDtypeStruct((n, d), BF16),
                   jax.ShapeDtypeStruct((n + tm, _packed_row_words(d)), jnp.int32),
                   jax.ShapeDtypeStruct((n, e), F32)],
        compiler_params=_params("arbitrary"),
        name="modulate_route",
    )(x, gain.reshape(1, d), shift, scale, w_router)


def _mm_body(*refs, n_w, n_extra, epilogue):
    a_ref = refs[0]
    w_refs = refs[1:1 + n_w]
    extra = refs[1 + n_w:1 + n_w + n_extra]
    o_ref = refs[1 + n_w + n_extra]
    w_bf = refs[2 + n_w + n_extra:]

    @pl.when(pl.program_id(1) == 0)
    def _():
        for w_ref, w_s in zip(w_refs, w_bf):
            w_s[...] = w_ref[...].astype(BF16)

    a = a_ref[...]
    accs = [jnp.dot(a, w_s[...], preferred_element_type=F32) for w_s in w_bf]
    o_ref[...] = epilogue(*accs, *[r[...] for r in extra]).astype(o_ref.dtype)


def _stacked(w, *lead):
    flat = 0
    for size, idx in zip(w.shape[:len(lead)], lead):
        flat = flat * size + idx
    return w.reshape((-1,) + w.shape[len(lead):]), flat


def matmul(name, a, ws, epilogue=None, extra=(), out_dtype=F32, tm=None, tn=None):
    m, k = a.shape
    n = ws[0][0].shape[2]
    tm = min(tm or MM_TM, m)
    tn = min(tn or MM_TN, n)
    assert m % tm == 0 and n % tn == 0, (name, m, n, tm, tn)
    if epilogue is None:
        epilogue = lambda acc: acc
    body = functools.partial(_mm_body, n_w=len(ws), n_extra=len(extra), epilogue=epilogue)

    def w_spec(idx):
        return pl.BlockSpec((None, k, tn), lambda j, i: (idx, 0, j))

    return pl.pallas_call(
        body,
        grid=(n // tn, m // tm),
        in_specs=([pl.BlockSpec((tm, k), lambda j, i: (i, 0))]
                  + [w_spec(idx) for _, idx in ws]
                  + [spec for _, spec in extra]),
        out_specs=pl.BlockSpec((tm, tn), lambda j, i: (i, j)),
        out_shape=jax.ShapeDtypeStruct((m, n), out_dtype),
        scratch_shapes=[pltpu.VMEM((k, tn), BF16) for _ in ws],
        compiler_params=_params("arbitrary", "arbitrary"),
        name=name,
    )(a, *[w for w, _ in ws], *[arr for arr, _ in extra])


def _row_vec_spec(tn):
    return pl.BlockSpec((1, tn), lambda j, i: (0, j))


def _gate_spec(tm, tn):
    return pl.BlockSpec((None, 1, tn), lambda j, i: (_group_of_block(i, tm), 0, j))


def _tile_spec(tm, tn):
    return pl.BlockSpec((tm, tn), lambda j, i: (i, j))


def _rms_head(x, gain):
    return x * lax.rsqrt(jnp.mean(x * x, axis=-1, keepdims=True) + NORM_EPS) * gain


def _rope(x, cos_f, sin_f):
    return x * cos_f + pltpu.roll(x, HEAD_DIM // 2, 1) * sin_f


def _attn_ctx_body(q_ref, k_ref, v_ref, qg_ref, kg_ref, o_ref, kc_ref, vc_ref):
    kn = _rms_head(k_ref[...], kg_ref[...])
    kc_ref[...] = kn
    v = v_ref[...]
    vc_ref[...] = v
    kb = kn.astype(BF16)
    vb = v.astype(BF16)
    scale = HEAD_DIM ** -0.5
    for h in range(N_HEADS // N_KV_HEADS):
        sl = slice(h * HEAD_DIM, (h + 1) * HEAD_DIM)
        qh = _rms_head(q_ref[:, sl], qg_ref[...]).astype(BF16)
        s = lax.dot_general(qh, kb, (((1,), (1,)), ((), ())), preferred_element_type=F32) * scale
        p = jnp.exp(s - jnp.max(s, axis=-1, keepdims=True))
        l = jnp.sum(p, axis=-1, keepdims=True)
        o = jnp.dot(p.astype(BF16), vb, preferred_element_type=F32) / l
        o_ref[:, sl] = o.astype(o_ref.dtype)


def attention_context(qkv, q_gain, k_gain):
    n = _n_ctx()
    g = N_HEADS // N_KV_HEADS
    hd = HEAD_DIM
    return pl.pallas_call(
        _attn_ctx_body,
        grid=(BATCH, N_KV_HEADS),
        in_specs=[pl.BlockSpec((SEQ, g * hd), lambda b, kh: (b, kh)),
                  pl.BlockSpec((SEQ, hd), lambda b, kh: (b, N_HEADS + kh)),
                  pl.BlockSpec((SEQ, hd), lambda b, kh: (b, N_HEADS + N_KV_HEADS + kh)),
                  pl.BlockSpec((1, hd), lambda b, kh: (0, 0)),
                  pl.BlockSpec((1, hd), lambda b, kh: (0, 0))],
        out_specs=[pl.BlockSpec((SEQ, g * hd), lambda b, kh: (b, kh)),
                   pl.BlockSpec((SEQ, hd), lambda b, kh: (b, kh)),
                   pl.BlockSpec((SEQ, hd), lambda b, kh: (b, kh))],
        out_shape=[jax.ShapeDtypeStruct((n, N_HEADS * hd), BF16),
                   jax.ShapeDtypeStruct((_n_ctx(), N_KV_HEADS * hd), F32),
                   jax.ShapeDtypeStruct((_n_ctx(), N_KV_HEADS * hd), F32)],
        compiler_params=_params("arbitrary", "arbitrary"),
        name="attn_ctx",
    )(qkv, qkv, qkv, q_gain.reshape(1, hd), k_gain.reshape(1, hd))


def _attn_lat_body(q_ref, k_ref, v_ref, ck_ref, cv_ref, cosq_ref, sinq_ref, cosk_ref, sink_ref,
                   qg_ref, kg_ref, o_ref):
    kn = _rope(_rms_head(k_ref[...], kg_ref[...]), cosk_ref[...], sink_ref[...]).astype(BF16)
    vb = v_ref[...].astype(BF16)
    ckb = ck_ref[...].astype(BF16)
    cvb = cv_ref[...].astype(BF16)
    scale = HEAD_DIM ** -0.5
    nt = (((1,), (1,)), ((), ()))
    for h in range(N_HEADS // N_KV_HEADS):
        sl = slice(h * HEAD_DIM, (h + 1) * HEAD_DIM)
        qh = _rope(_rms_head(q_ref[:, sl], qg_ref[...]), cosq_ref[...], sinq_ref[...]).astype(BF16)
        s_c = lax.dot_general(qh, ckb, nt, preferred_element_type=F32) * scale
        s_l = lax.dot_general(qh, kn, nt, preferred_element_type=F32) * scale
        m = jnp.maximum(jnp.max(s_c, axis=-1, keepdims=True), jnp.max(s_l, axis=-1, keepdims=True))
        p_c = jnp.exp(s_c - m)
        p_l = jnp.exp(s_l - m)
        l = jnp.sum(p_c, axis=-1, keepdims=True) + jnp.sum(p_l, axis=-1, keepdims=True)
        o = (jnp.dot(p_c.astype(BF16), cvb, preferred_element_type=F32)
             + jnp.dot(p_l.astype(BF16), vb, preferred_element_type=F32)) / l
        o_ref[:, sl] = o.astype(o_ref.dtype)


def attention_latent(qkv, cache_k, cache_v, cos_f, sin_f, q_gain, k_gain):
    g = N_HEADS // N_KV_HEADS
    hd = HEAD_DIM
    tq = min(ATTN_TQ, DEC_SEQ)
    nq = DEC_SEQ // tq
    q_row0 = _n_ctx() // tq
    k_row0 = _n_ctx() // DEC_SEQ
    q_map = lambda b, kh, qi: (q_row0 + b * nq + qi, kh)
    return pl.pallas_call(
        _attn_lat_body,
        grid=(DEC_BATCH, N_KV_HEADS, nq),
        in_specs=[pl.BlockSpec((tq, g * hd), q_map),
                  pl.BlockSpec((DEC_SEQ, hd), lambda b, kh, qi: (k_row0 + b, N_HEADS + kh)),
                  pl.BlockSpec((DEC_SEQ, hd), lambda b, kh, qi: (k_row0 + b, N_HEADS + N_KV_HEADS + kh)),
                  pl.BlockSpec((None, None, PAST_LEN, hd), lambda b, kh, qi: (b, kh, 0, 0)),
                  pl.BlockSpec((None, None, PAST_LEN, hd), lambda b, kh, qi: (b, kh, 0, 0)),
                  pl.BlockSpec((tq, hd), lambda b, kh, qi: (qi, 0)),
                  pl.BlockSpec((tq, hd), lambda b, kh, qi: (qi, 0)),
                  pl.BlockSpec((DEC_SEQ, hd), lambda b, kh, qi: (0, 0)),
                  pl.BlockSpec((DEC_SEQ, hd), lambda b, kh, qi: (0, 0)),
                  pl.BlockSpec((1, hd), lambda b, kh, qi: (0, 0)),
                  pl.BlockSpec((1, hd), lambda b, kh, qi: (0, 0))],
        out_specs=pl.BlockSpec((tq, g * hd), lambda b, kh, qi: (b * nq + qi, kh)),
        out_shape=jax.ShapeDtypeStruct((DEC_BATCH * DEC_SEQ, N_HEADS * hd), BF16),
        compiler_params=_params("arbitrary", "arbitrary", "arbitrary"),
        name="attn_lat",
    )(qkv, qkv, qkv, cache_k, cache_v, cos_f, sin_f, cos_f, sin_f,
      q_gain.reshape(1, hd), k_gain.reshape(1, hd))


def rope_tables():
    rows = DEC_SEQ // GRID_W
    r_idx, c_idx = jnp.meshgrid(jnp.arange(rows), jnp.arange(GRID_W), indexing="ij")
    r_idx = r_idx.reshape(-1).astype(F32)
    c_idx = c_idx.reshape(-1).astype(F32)
    n_freq = HEAD_DIM // 4
    inv = ROPE_THETA ** (-jnp.arange(n_freq, dtype=F32) / n_freq)
    ang = jnp.concatenate([r_idx[:, None] * inv, c_idx[:, None] * inv], axis=-1)
    cos, sin = jnp.cos(ang), jnp.sin(ang)
    return jnp.concatenate([cos, cos], axis=-1), jnp.concatenate([-sin, sin], axis=-1)


def _mix_body(x_ref, xp_ref, xn_ref, gain_ref, sh_ref, sc_ref, mu_ref, *o_refs):
    i = pl.program_id(0)
    tm = x_ref.shape[0]
    n_ctx_blocks = _n_ctx() // tm
    per_seq = DEC_SEQ // tm
    j = (i - n_ctx_blocks) % per_seq
    is_first = jnp.logical_or(i < n_ctx_blocks, j == 0)
    is_last = jnp.logical_or(i < n_ctx_blocks, j == per_seq - 1)
    gain, sh, sc = gain_ref[...], sh_ref[...], sc_ref[...]
    h = _modulated(x_ref[...], gain, sh, sc)
    h_prev = jnp.where(is_first, 0.0, _modulated(xp_ref[7:8, :], gain, sh, sc))
    h_next = jnp.where(is_last, 0.0, _modulated(xn_ref[0:1, :], gain, sh, sc))
    row = lax.broadcasted_iota(jnp.int32, h.shape, 0)
    prev = jnp.where(row == 0, h_prev, pltpu.roll(h, 1, 0))
    nxt = jnp.where(row == tm - 1, h_next, pltpu.roll(h, tm - 1, 0))
    xx = 0.5 * (prev + nxt) - h
    for n, o_ref in enumerate(o_refs):
        o_ref[...] = (h + xx * mu_ref[n:n + 1, :]).astype(o_ref.dtype)


def rwkv_mix(x, gain, shift, scale, mu):
    n, d = x.shape
    tm = ROW_BLOCK
    rb = tm // 8
    last8 = n // 8 - 1
    return pl.pallas_call(
        _mix_body,
        grid=(n // tm,),
        in_specs=[pl.BlockSpec((tm, d), lambda i: (i, 0)),
                  pl.BlockSpec((8, d), lambda i: (jnp.maximum(i * rb - 1, 0), 0)),
                  pl.BlockSpec((8, d), lambda i: (jnp.minimum((i + 1) * rb, last8), 0)),
                  pl.BlockSpec((1, d), lambda i: (0, 0)),
                  _group_spec(d, tm), _group_spec(d, tm),
                  pl.BlockSpec((mu.shape[0], d), lambda i: (0, 0))],
        out_specs=[pl.BlockSpec((tm, d), lambda i: (i, 0)) for _ in range(6)],
        out_shape=[jax.ShapeDtypeStruct((n, d), BF16) for _ in range(6)],
        compiler_params=_params("arbitrary"),
        name="rwkv_mix",
    )(x, x, x, gain.reshape(1, d), shift, scale, mu)


def _seg_sum(x, ones_bd):
    hi, lo = _split_bf16(x)
    return (jnp.dot(hi, ones_bd, preferred_element_type=F32)
            + jnp.dot(lo, ones_bd, preferred_element_type=F32))


def _scan_body(r_ref, k_ref, v_ref, wl_ref, al_ref, g_ref, kk_ref, ka_ref, rk_ref, lnw_ref, lnb_ref,
               s0_ref, *rest, reverse, add_prev):
    if add_prev:
        prev_ref, o_ref, sfin_ref, s_scr = rest
    else:
        o_ref, sfin_ref, s_scr = rest
    L = SCAN_CHUNK
    R = 2 * L
    hw = RWKV_HEAD
    n_pairs = s_scr.shape[0]
    n_il = min(SCAN_INTERLEAVE, n_pairs)
    step = pl.program_id(0)
    n_chunks = pl.num_programs(0)
    c = (n_chunks - 1 - step) if reverse else step
    ctx_chunks = _n_ctx() // L
    per_ctx, per_lat = SEQ // L, DEC_SEQ // L
    pos = jnp.where(c < ctx_chunks, c % per_ctx, (c - ctx_chunks) % per_lat)
    seq_len = jnp.where(c < ctx_chunks, per_ctx, per_lat)
    first_pos = seq_len - 1 if reverse else 0
    last_pos = 0 if reverse else seq_len - 1

    is_ctx = c < ctx_chunks

    @pl.when(jnp.logical_and(pos == first_pos, is_ctx))
    def _():
        s_scr[...] = jnp.zeros(s_scr.shape, F32)

    @pl.when(jnp.logical_and(pos == first_pos, jnp.logical_not(is_ctx)))
    def _():
        s_scr[...] = s0_ref[...]

    ri = lax.broadcasted_iota(jnp.int32, (L, L), 0)
    ci = lax.broadcasted_iota(jnp.int32, (L, L), 1)
    before = (ci > ri) if reverse else (ci < ri)
    incl_bf = jnp.logical_or(before, ci == ri).astype(BF16)
    r4 = lax.broadcasted_iota(jnp.int32, (2 * R, 2 * R), 0)
    c4 = lax.broadcasted_iota(jnp.int32, (2 * R, 2 * R), 1)
    same_head = ((r4 % R) // L) == ((c4 % R) // L)
    t4, j4 = r4 % L, c4 % L
    before4 = (j4 > t4) if reverse else (j4 < t4)
    mask4 = jnp.logical_and(same_head, jnp.logical_or(before4, jnp.logical_and(r4 >= R, j4 == t4)))
    eye = (lax.broadcasted_iota(jnp.int32, (R, R), 0) == lax.broadcasted_iota(jnp.int32, (R, R), 1)).astype(F32)
    lr = lax.broadcasted_iota(jnp.int32, (LANES, LANES), 0)
    lc = lax.broadcasted_iota(jnp.int32, (LANES, LANES), 1)
    ones_bd = ((lr // hw) == (lc // hw)).astype(BF16)
    head_a = lax.broadcasted_iota(jnp.int32, (L, LANES), 1) < hw
    nt = (((1,), (1,)), ((), ()))
    inv_hw = 1.0 / hw
    Q = range(n_il)

    def stack(x):
        return jnp.concatenate([jnp.where(head_a, x, 0.0), jnp.where(head_a, 0.0, x)], axis=0)

    def mm(x, y):
        return jnp.dot(x.astype(BF16), y.astype(BF16), preferred_element_type=F32)

    def group(gi, carry):
        sls = [pl.ds(pl.multiple_of((gi * n_il + q) * LANES, LANES), LANES) for q in Q]
        s_vk = [s_scr[gi * n_il + q] for q in Q]
        r = [r_ref[:, sl] for sl in sls]
        k = [k_ref[:, sl] for sl in sls]
        v = [v_ref[:, sl] for sl in sls]
        lw = [DECAY_SCALE * jax.nn.sigmoid(wl_ref[:, sl]) for sl in sls]
        ag = [jax.nn.sigmoid(al_ref[:, sl]) for sl in sls]
        kkf = [k[q] * kk_ref[:, sls[q]] for q in Q]
        nrm = [_seg_sum(kkf[q] * kkf[q], ones_bd) for q in Q]
        kk = [kkf[q] / jnp.maximum(jnp.sqrt(nrm[q]), 1e-12) for q in Q]
        kd = [k[q] * (1.0 + (ag[q] - 1.0) * ka_ref[:, sls[q]]) for q in Q]
        cs = []
        for q in Q:
            w_hi = lw[q].astype(BF16)
            w_r1 = lw[q] - w_hi.astype(F32)
            w_mid = w_r1.astype(BF16)
            w_lo = (w_r1 - w_mid.astype(F32)).astype(BF16)
            cs.append(jnp.dot(incl_bf, w_hi, preferred_element_type=F32)
                      + jnp.dot(incl_bf, w_mid, preferred_element_type=F32)
                      + jnp.dot(incl_bf, w_lo, preferred_element_type=F32))
        e_neg = [jnp.exp(-cs[q]) for q in Q]
        ar = [jnp.concatenate([stack(jnp.exp(cs[q] - lw[q]) * -kk[q]), stack(jnp.exp(cs[q]) * r[q])],
                              axis=0).astype(BF16) for q in Q]
        bk = [jnp.concatenate([stack(e_neg[q] * kk[q] * ag[q]), stack(e_neg[q] * kd[q])],
                              axis=0).astype(BF16) for q in Q]
        v_s = [stack(v[q]).astype(BF16) for q in Q]
        gm = [jnp.where(mask4, lax.dot_general(ar[q], bk[q], nt, preferred_element_type=F32), 0.0) for q in Q]
        a_s = [lax.dot_general(ar[q], s_vk[q].astype(BF16), nt, preferred_element_type=F32) for q in Q]
        w0 = [a_s[q][:R] + mm(gm[q][:R, R:], v_s[q]) for q in Q]
        m = [eye + gm[q][:R, :R] for q in Q]
        pw = [gm[q][:R, :R] for q in Q]
        n = 1
        while n < L // 2:
            pw = [mm(pw[q], pw[q]) for q in Q]
            m = [m[q] + mm(m[q], pw[q]) for q in Q]
            n *= 2
        u_s = [mm(m[q], w0[q]) for q in Q]
        uv = [jnp.concatenate([u_s[q], v_s[q].astype(F32)], axis=0) for q in Q]
        y_s = [a_s[q][R:] + mm(gm[q][R:, :], uv[q]) for q in Q]
        y = [y_s[q][:L] + y_s[q][L:] for q in Q]
        upd = [mm(uv[q].T, bk[q]) for q in Q]
        for q in Q:
            c_tot = cs[q][0:1, :] if reverse else cs[q][L - 1:L, :]
            s_scr[gi * n_il + q] = (s_vk[q] + upd[q]) * jnp.exp(c_tot)
        mean = [_seg_sum(y[q], ones_bd) * inv_hw for q in Q]
        dy = [y[q] - mean[q] for q in Q]
        var = [_seg_sum(dy[q] * dy[q], ones_bd) * inv_hw for q in Q]
        bonus = [_seg_sum(r[q] * kd[q] * rk_ref[:, sls[q]], ones_bd) * v[q] for q in Q]
        for q in Q:
            sl = sls[q]
            yn = dy[q] * lax.rsqrt(var[q] + GN_EPS) * lnw_ref[:, sl] + lnb_ref[:, sl]
            out = (yn + bonus[q]) * g_ref[:, sl]
            if add_prev:
                out = out + prev_ref[:, sl].astype(F32)
            o_ref[:, sl] = out.astype(o_ref.dtype)
        return carry

    lax.fori_loop(0, n_pairs // n_il, group, 0)

    @pl.when(jnp.logical_and(pos == last_pos, is_ctx))
    def _():
        for p in range(n_pairs):
            s_pair = s_scr[p]
            sfin_ref[2 * p] = s_pair[:hw, :hw]
            sfin_ref[2 * p + 1] = pltpu.roll(s_pair[hw:, :], hw, 1)[:, :hw]


def wkv_scan_dir(r, k, v, wl, al, g, k_k, k_a, r_k, lnx_w, lnx_b, s0, reverse, prev=None):
    n, d = r.shape
    L = SCAN_CHUNK
    n_chunks = n // L
    n_pairs = d // LANES
    ctx_chunks = _n_ctx() // L

    def cidx(s):
        return (n_chunks - 1 - s) if reverse else s

    def seq_of(s):
        c = cidx(s)
        return jnp.where(c < ctx_chunks, c // (SEQ // L), BATCH + (c - ctx_chunks) // (DEC_SEQ // L))

    tok = pl.BlockSpec((L, d), lambda s: (cidx(s), 0))
    vec = pl.BlockSpec((1, d), lambda s: (0, 0))
    hw = RWKV_HEAD
    st_in = pl.BlockSpec((None, n_pairs, LANES, LANES),
                         lambda s: (jnp.clip(seq_of(s) - BATCH, 0, DEC_BATCH - 1), 0, 0, 0))
    st_out = pl.BlockSpec((None, d // hw, hw, hw), lambda s: (jnp.minimum(seq_of(s), BATCH - 1), 0, 0, 0))
    add_prev = prev is not None
    body = functools.partial(_scan_body, reverse=reverse, add_prev=add_prev)
    args = [r, k, v, wl, al, g, k_k.reshape(1, d), k_a.reshape(1, d), r_k.reshape(1, d),
            lnx_w.reshape(1, d), lnx_b.reshape(1, d), s0]
    in_specs = [tok] * 6 + [vec] * 5 + [st_in]
    if add_prev:
        args.append(prev)
        in_specs.append(tok)
    return pl.pallas_call(
        body,
        grid=(n_chunks,),
        in_specs=in_specs,
        out_specs=[tok, st_out],
        out_shape=[jax.ShapeDtypeStruct((n, d), BF16 if add_prev else F32),
                   jax.ShapeDtypeStruct((BATCH, d // hw, hw, hw), F32)],
        scratch_shapes=[pltpu.VMEM((n_pairs, LANES, LANES), F32)],
        compiler_params=_params("arbitrary"),
        name="wkv_bwd" if reverse else "wkv_fwd",
    )(*args)


def _pair_states(s):
    n_seq, h, hv, hk = s.shape
    s = s.reshape(n_seq, h // 2, 2, hv, hk)
    z = jnp.zeros_like(s[:, :, 0])
    top = jnp.concatenate([s[:, :, 0], z], axis=-1)
    bot = jnp.concatenate([z, s[:, :, 1]], axis=-1)
    return jnp.concatenate([top, bot], axis=-2)


ITEM_BLOCKS = 4


def _store_rows(o_ref, value):
    rows = value.shape[0]
    o_ref[:rows, :] = value.astype(o_ref.dtype)
    if rows < o_ref.shape[0]:
        o_ref[rows:, :] = jnp.zeros((o_ref.shape[0] - rows, o_ref.shape[1]), o_ref.dtype)


def _expert_up_body(ie_ref, ib_ref, inb_ref, last_ref, *refs):
    del ie_ref, ib_ref, last_ref
    x_refs = refs[:ITEM_BLOCKS]
    wg_ref, wu_ref, o_ref = refs[ITEM_BLOCKS:]
    nb = inb_ref[pl.program_id(0)]
    half = x_refs[0].shape[1]
    for n_sub in range(1, ITEM_BLOCKS + 1):
        @pl.when(nb == n_sub)
        def _(n_sub=n_sub):
            xp = jnp.concatenate([x_refs[s][...] for s in range(n_sub)], axis=0)
            x_lo, x_hi = [v.astype(BF16) for v in _unpack_bf16_pairs(xp)]
            wg = wg_ref[...].astype(BF16)
            wu = wu_ref[...].astype(BF16)
            gate = (jnp.dot(x_lo, wg[:half], preferred_element_type=F32)
                    + jnp.dot(x_hi, wg[half:], preferred_element_type=F32))
            up = (jnp.dot(x_lo, wu[:half], preferred_element_type=F32)
                  + jnp.dot(x_hi, wu[half:], preferred_element_type=F32))
            _store_rows(o_ref, _silu(gate) * up)


def _item_x_spec(s, bm, half):
    def index(it, f, ie, ib, inb, last):
        return ib[it] + jnp.minimum(s, jnp.maximum(inb[it] - 1, 0)), 0
    return pl.BlockSpec((bm, half), index)


def expert_up(xs, w_gate, w_up, layer, item_e, item_blk, item_nb, last_item, tf=256):
    d = w_gate.shape[2]
    half = d // 2
    n_exp = w_gate.shape[1]
    ff = w_gate.shape[-1]
    w_gate = w_gate.reshape((-1,) + w_gate.shape[2:])
    w_up = w_up.reshape((-1,) + w_up.shape[2:])
    bm = ROW_BLOCK
    tf = min(tf, ff)
    n_f = ff // tf
    n_items = item_e.shape[0]
    item_rows = ITEM_BLOCKS * bm

    def col(it, f, last):
        return jnp.where(it > last[0], n_f - 1, f)

    w_spec = pl.BlockSpec((None, d, tf),
                          lambda it, f, ie, ib, inb, last: (layer * n_exp + ie[it], 0, col(it, f, last)))
    return pl.pallas_call(
        _expert_up_body,
        grid_spec=pltpu.PrefetchScalarGridSpec(
            num_scalar_prefetch=4,
            grid=(n_items, n_f),
            in_specs=[_item_x_spec(s, bm, half) for s in range(ITEM_BLOCKS)] + [w_spec, w_spec],
            out_specs=pl.BlockSpec((item_rows, tf),
                                   lambda it, f, ie, ib, inb, last: (jnp.minimum(it, last[0]), col(it, f, last)))),
        out_shape=jax.ShapeDtypeStruct((n_items * item_rows, ff), BF16),
        compiler_params=_params("arbitrary", "arbitrary"),
        name="expert_up",
    )(item_e, item_blk, item_nb, last_item, *([xs] * ITEM_BLOCKS), w_gate, w_up)


def _expert_down_body(ie_ref, ib_ref, inb_ref, last_ref, h_ref, wd_ref, o_ref):
    del ie_ref, ib_ref, last_ref
    nb = inb_ref[pl.program_id(1)]
    bm = h_ref.shape[0] // ITEM_BLOCKS
    for n_sub in range(1, ITEM_BLOCKS + 1):
        @pl.when(nb == n_sub)
        def _(n_sub=n_sub):
            rows = n_sub * bm
            y = jnp.dot(h_ref[:rows, :], wd_ref[...].astype(BF16), preferred_element_type=F32)
            pc = _pack_cols(y.shape[1])
            tw = pc // 2 + PACK_PAD
            for t in range(y.shape[1] // pc):
                o_ref[:rows, t * tw:t * tw + pc // 2] = _pack_bf16_pairs(
                    y[:, t * pc:t * pc + pc // 2], y[:, t * pc + pc // 2:(t + 1) * pc])
                o_ref[:rows, t * tw + pc // 2:(t + 1) * tw] = jnp.zeros((rows, PACK_PAD), jnp.int32)
            if rows < o_ref.shape[0]:
                o_ref[rows:, :] = jnp.zeros((o_ref.shape[0] - rows, o_ref.shape[1]), jnp.int32)


def expert_down(hm, w_down, layer, item_e, item_blk, item_nb, last_item, tn=2048):
    n_rows, ff = hm.shape
    d = w_down.shape[-1]
    n_exp = w_down.shape[1]
    w_down = w_down.reshape((-1,) + w_down.shape[2:])
    tn = min(tn, d)
    n_items = item_e.shape[0]
    item_rows = n_rows // n_items
    out_tn = tn // _pack_cols(d) * _pack_tile_words(d)
    return pl.pallas_call(
        _expert_down_body,
        grid_spec=pltpu.PrefetchScalarGridSpec(
            num_scalar_prefetch=4,
            grid=(d // tn, n_items),
            in_specs=[pl.BlockSpec((item_rows, ff), lambda j, it, ie, ib, inb, last: (jnp.minimum(it, last[0]), 0)),
                      pl.BlockSpec((None, ff, tn), lambda j, it, ie, ib, inb, last: (layer * n_exp + ie[it], 0, j))],
            out_specs=pl.BlockSpec((item_rows, out_tn), lambda j, it, ie, ib, inb, last: (jnp.minimum(it, last[0]), j))),
        out_shape=jax.ShapeDtypeStruct((n_rows, d // tn * out_tn), jnp.int32),
        compiler_params=_params("arbitrary", "arbitrary"),
        name="expert_down",
    )(item_e, item_blk, item_nb, last_item, hm, w_down)


def _moe_out_body(hs_ref, w_ref, *refs):
    n_k = len(refs) - 5
    g_refs = refs[:n_k]
    gw_ref, res_ref, gate_ref, o_ref, w_bf = refs[n_k:]

    @pl.when(pl.program_id(1) == 0)
    def _():
        w_bf[...] = w_ref[...].astype(BF16)

    shared = jnp.dot(hs_ref[...], w_bf[...], preferred_element_type=F32)
    tm, pc = shared.shape
    half = pc // 2
    gw = gw_ref[...]
    lo = jnp.zeros((tm, half), F32)
    hi = jnp.zeros((tm, half), F32)
    for k, g_ref in enumerate(g_refs):
        w_lo, w_hi = _unpack_bf16_pairs(g_ref[:, :half])
        lo = lo + gw[:, k:k + 1] * w_lo
        hi = hi + gw[:, k:k + 1] * w_hi
    res = res_ref[...]
    gate = gate_ref[...]
    o_ref[:, :half] = res[:, :half] + gate[:, :half] * (shared[:, :half] + lo)
    o_ref[:, half:] = res[:, half:] + gate[:, half:] * (shared[:, half:] + hi)


def moe_out(hs, w_down, gathered, gw, resid, gate):
    n, ff = hs.shape
    w_stack, w_idx = w_down
    d = w_stack.shape[2]
    n_k = gw.shape[1]
    tm = ROW_BLOCK
    pc = _pack_cols(d)
    tw = _pack_tile_words(d)

    def g_spec(k):
        return pl.BlockSpec((tm, tw), lambda j, i: (k * (n // tm) + i, j))

    return pl.pallas_call(
        _moe_out_body,
        grid=(d // pc, n // tm),
        in_specs=[pl.BlockSpec((tm, ff), lambda j, i: (i, 0)),
                  pl.BlockSpec((None, ff, pc), lambda j, i: (w_idx, 0, j))]
                 + [g_spec(k) for k in range(n_k)]
                 + [pl.BlockSpec((tm, n_k), lambda j, i: (i, 0)),
                  pl.BlockSpec((tm, pc), lambda j, i: (i, j)),
                  pl.BlockSpec((None, 1, pc), lambda j, i: (_group_of_block(i, tm), 0, j))],
        out_specs=pl.BlockSpec((tm, pc), lambda j, i: (i, j)),
        out_shape=jax.ShapeDtypeStruct((n, d), F32),
        scratch_shapes=[pltpu.VMEM((ff, pc), BF16)],
        compiler_params=_params("arbitrary", "arbitrary"),
        name="moe_out",
    )(hs, w_stack, *([gathered] * n_k), gw, resid, gate)


def route(scores, b_router):
    n = scores.shape[0]
    bm = ROW_BLOCK
    _, idx = lax.top_k(scores + b_router.astype(F32), TOP_K)
    sel = jnp.take_along_axis(scores, idx, axis=-1)
    gw = sel / jnp.sum(sel, axis=-1, keepdims=True) * ROUTED_SCALE
    a = n * TOP_K
    n_blocks = a // bm + N_EXPERTS
    e_flat = idx.reshape(-1)
    onehot = (e_flat[:, None] == jnp.arange(N_EXPERTS)[None, :]).astype(jnp.int32)
    rank = jnp.take_along_axis(jnp.cumsum(onehot, axis=0), e_flat[:, None], axis=1)[:, 0] - 1
    counts = jnp.sum(onehot, axis=0)
    padded = (counts + bm - 1) // bm * bm
    pad_end = jnp.cumsum(padded)
    pad_start = pad_end - padded
    dest = (pad_start[e_flat] + rank).astype(jnp.int32)
    tok_flat = jnp.arange(a, dtype=jnp.int32) // TOP_K
    slot_tok = jnp.full((n_blocks * bm,), n, jnp.int32).at[dest].set(tok_flat)
    ib = ITEM_BLOCKS
    n_items = n_blocks // ib + N_EXPERTS
    e_blocks = padded // bm
    e_items = (e_blocks + ib - 1) // ib
    item_end = jnp.cumsum(e_items)
    item_begin = item_end - e_items
    last_item = (item_end[-1:] - 1).astype(jnp.int32)
    it = jnp.minimum(jnp.arange(n_items), last_item[0])
    item_e = jnp.minimum(jnp.searchsorted(item_end, it, side="right"), N_EXPERTS - 1).astype(jnp.int32)
    local = it - item_begin[item_e]
    item_blk = (pad_start[item_e] // bm + ib * local).astype(jnp.int32)
    item_nb = jnp.where(jnp.arange(n_items) <= last_item[0],
                        jnp.clip(e_blocks[item_e] - ib * local, 0, ib), 0).astype(jnp.int32)
    item_rows = ib * bm
    out_row = ((item_begin[e_flat] + rank // item_rows) * item_rows + rank % item_rows).astype(jnp.int32)
    return gw, out_row.reshape(n, TOP_K), slot_tok, item_e, item_blk, item_nb, last_item


def kernel(x_prompt, x_sample, c, cache_k, cache_v, state_wkv, c_ctx, w_ada, b_ada, norm_mix, norm_ffn, attn_w_qkv, attn_w_o, attn_q_gain, attn_k_gain, rw_mu, rw_w_r, rw_w_k, rw_w_v, rw_w_o, rw_w0, rw_w1, rw_w2, rw_a0, rw_a1, rw_a2, rw_g1, rw_g2, rw_k_k, rw_k_a, rw_r_k, rw_lnx_w, rw_lnx_b, moe_w_router, moe_b_router, moe_w_gate, moe_w_up, moe_w_down, sh_w_gate, sh_w_up, sh_w_down):
    d = D_MODEL
    n_ctx, n = _n_ctx(), _n_tok()
    n_grp = 1 + DEC_BATCH
    x = jnp.concatenate([x_prompt.reshape(n_ctx, d), x_sample.reshape(n - n_ctx, d)], axis=0)

    cvec8 = jnp.zeros((8, d), F32).at[0].set(c_ctx).at[1:n_grp].set(c)
    mods = ada_all(cvec8, w_ada, b_ada)

    def mod(layer, m):
        return mods[layer, :n_grp, m * d:(m + 1) * d].reshape(n_grp, 1, d)

    tm, tn = min(MM_TM, n), min(MM_TN, d)
    resid_gate = lambda acc, res, gate: res + gate * acc

    new_k = new_v = new_s = None
    for i in range(DEPTH):
        j = i // 2
        sh1, sc1, g1, sh2, sc2, g2 = [mod(i, m) for m in range(N_MOD)]
        if i % 2 == 0:
            h = modulate(x, norm_mix[i], sh1, sc1)
            qkv = matmul("qkv", h, [_stacked(attn_w_qkv, j)])
            o_ctx, kc, vc = attention_context(qkv, attn_q_gain[j], attn_k_gain[j])
            cos_f, sin_f = rope_tables()
            o_lat = attention_latent(qkv, jnp.swapaxes(cache_k[:, j], 1, 2), jnp.swapaxes(cache_v[:, j], 1, 2),
                                     cos_f, sin_f, attn_q_gain[j], attn_k_gain[j])
            o = jnp.concatenate([o_ctx, o_lat], axis=0)
            new_k = kc.reshape(BATCH, 1, SEQ, N_KV_HEADS, HEAD_DIM)
            new_v = vc.reshape(BATCH, 1, SEQ, N_KV_HEADS, HEAD_DIM)
            w_o = _stacked(attn_w_o, j)
        else:
            xr, xw, xk, xv, xa, xg = rwkv_mix(x, norm_mix[i], sh1, sc1, rw_mu[j])
            r = matmul("rwkv_r", xr, [_stacked(rw_w_r, j)])
            k = matmul("rwkv_k", xk, [_stacked(rw_w_k, j)])
            v = matmul("rwkv_v", xv, [_stacked(rw_w_v, j)])
            lw = DECAY_LORA
            la = AAA_LORA
            lg = GATE_LORA
            lgp = -(-lg // LANES) * LANES
            w_mid = matmul("decay_lora_in", xw, [(jnp.concatenate([rw_w1[j, 0], rw_w1[j, 1]], axis=1)[None], 0)],
                           epilogue=jnp.tanh, out_dtype=BF16)
            a_mid = matmul("rate_lora_in", xa, [(jnp.concatenate([rw_a1[j, 0], rw_a1[j, 1]], axis=1)[None], 0)],
                           out_dtype=BF16)
            g1p = jnp.pad(rw_g1[j], ((0, 0), (0, 0), (0, lgp - lg)))
            g2p = jnp.pad(rw_g2[j], ((0, 0), (0, lgp - lg), (0, 0)))
            g_mid = matmul("gate_lora_in", xg, [(jnp.concatenate([g1p[0], g1p[1]], axis=1)[None], 0)],
                           epilogue=jax.nn.sigmoid, out_dtype=BF16)
            add_vec = lambda acc, vec: acc + vec
            o = None
            finals = []
            for dr in range(2):
                wl = matmul("decay_lora_out", w_mid[:, dr * lw:(dr + 1) * lw], [_stacked(rw_w2, j, dr)],
                            epilogue=add_vec, extra=[(rw_w0[j, dr].reshape(1, d), _row_vec_spec(tn))])
                al = matmul("rate_lora_out", a_mid[:, dr * la:(dr + 1) * la], [_stacked(rw_a2, j, dr)],
                            epilogue=add_vec, extra=[(rw_a0[j, dr].reshape(1, d), _row_vec_spec(tn))])
                gg = matmul("gate_lora_out", g_mid[:, dr * lgp:(dr + 1) * lgp], [(g2p, dr)])
                o, s_fin = wkv_scan_dir(r, k, v, wl, al, gg, rw_k_k[j], rw_k_a[j], rw_r_k[j, dr],
                                        rw_lnx_w[j], rw_lnx_b[j], _pair_states(state_wkv[:, j, dr]),
                                        reverse=(dr == 1), prev=o)
                finals.append(s_fin)
            new_s = jnp.stack(finals, axis=1)[:, None]
            w_o = _stacked(rw_w_o, j)
        x = matmul("mixer_out", o, [w_o], epilogue=resid_gate,
                   extra=[(x, _tile_spec(tm, tn)), (g1, _gate_spec(tm, tn))])

        h2, h2_packed, scores = modulate_and_route(x, norm_ffn[i], sh2, sc2, moe_w_router[i])
        gw, out_row, slot_tok, *items = route(scores, moe_b_router[i])
        xs = jnp.take(h2_packed, slot_tok, axis=0, mode="clip")
        hs = matmul("shared_up", h2, [_stacked(sh_w_gate, i), _stacked(sh_w_up, i)],
                    epilogue=lambda a, b: _silu(a) * b, out_dtype=BF16, tn=256)
        xs, hs = lax.optimization_barrier((xs, hs))
        hm = expert_up(xs, moe_w_gate, moe_w_up, i, *items)
        yb = expert_down(hm, moe_w_down, i, *items)
        gathered = jnp.take(yb, out_row.T.reshape(-1), axis=0, mode="clip")
        x = moe_out(hs, _stacked(sh_w_down, i), gathered, gw, x, g2)

    y_prompt = x[:n_ctx].reshape(BATCH, SEQ, d)
    y_sample = x[n_ctx:].reshape(DEC_BATCH, DEC_SEQ, d)
    return (y_prompt, y_sample, new_k, new_v, new_s)
```

```python
import functools

import jax
import jax.numpy as jnp
from jax import lax
from jax.experimental import pallas as pl
from jax.experimental.pallas import tpu as pltpu

D_MODEL = 4096
BATCH = 16
SEQ = 256
DEPTH = 2
DEC_BATCH = 2
DEC_SEQ = 1024
PAST_LEN = 512
GRID_W = 64
N_HEADS = 32
N_KV_HEADS = 8
HEAD_DIM = 128
ROPE_THETA = 10000.0
RWKV_HEAD = 64
DECAY_LORA = 128
AAA_LORA = 128
GATE_LORA = 480
N_EXPERTS = 64
TOP_K = 8
EXPERT_FF = 1024
SHARED_FF = 1024
ROUTED_SCALE = 2.5
N_MOD = 6
NORM_EPS = 1e-6
GN_EPS = 64e-5
DECAY_SCALE = -0.606531

F32 = jnp.float32
BF16 = jnp.bfloat16

V7X_VMEM_LIMIT_BYTES = 56 * 1024 * 1024
LANES = 128
ROW_BLOCK = 256
MM_TM = 1024
MM_TN = 512
SCAN_CHUNK = 64
SCAN_INTERLEAVE = 16
ATTN_TQ = 512


def _params(*sem):
    return pltpu.CompilerParams(dimension_semantics=sem, vmem_limit_bytes=V7X_VMEM_LIMIT_BYTES)


def _n_ctx():
    return BATCH * SEQ


def _n_tok():
    return BATCH * SEQ + DEC_BATCH * DEC_SEQ


def _group_of_block(i, tm):
    n_ctx_blocks = _n_ctx() // tm
    return jnp.where(i < n_ctx_blocks, 0, 1 + (i - n_ctx_blocks) // (DEC_SEQ // tm))


def _silu(x):
    return x * jax.nn.sigmoid(x)


def _ada_body(c_ref, w_ref, b_ref, o_ref):
    s = _silu(c_ref[...]).astype(BF16)
    o_ref[...] = jnp.dot(s, w_ref[...].astype(BF16), preferred_element_type=F32) + b_ref[...]


def ada_all(cvec8, w_ada, b_ada, tn=512):
    depth, d, n6 = w_ada.shape
    return pl.pallas_call(
        _ada_body,
        grid=(depth, n6 // tn),
        in_specs=[pl.BlockSpec((8, d), lambda l, j: (0, 0)),
                  pl.BlockSpec((None, d, tn), lambda l, j: (l, 0, j)),
                  pl.BlockSpec((None, 1, tn), lambda l, j: (l, 0, j))],
        out_specs=pl.BlockSpec((None, 8, tn), lambda l, j: (l, 0, j)),
        out_shape=jax.ShapeDtypeStruct((depth, 8, n6), F32),
        compiler_params=_params("arbitrary", "arbitrary"),
        name="ada",
    )(cvec8, w_ada, b_ada.reshape(depth, 1, n6))


def _modulated(x, gain, shift, scale):
    y = x * lax.rsqrt(jnp.mean(x * x, axis=-1, keepdims=True) + NORM_EPS) * gain
    return y * (1.0 + scale) + shift


def _modulate_body(x_ref, gain_ref, sh_ref, sc_ref, o_ref):
    o_ref[...] = _modulated(x_ref[...], gain_ref[...], sh_ref[...], sc_ref[...]).astype(o_ref.dtype)


def _group_spec(d, tm):
    return pl.BlockSpec((None, 1, d), lambda i: (_group_of_block(i, tm), 0, 0))


def modulate(x, gain, shift, scale):
    n, d = x.shape
    tm = ROW_BLOCK
    return pl.pallas_call(
        _modulate_body,
        grid=(n // tm,),
        in_specs=[pl.BlockSpec((tm, d), lambda i: (i, 0)),
                  pl.BlockSpec((1, d), lambda i: (0, 0)),
                  _group_spec(d, tm), _group_spec(d, tm)],
        out_specs=pl.BlockSpec((tm, d), lambda i: (i, 0)),
        out_shape=jax.ShapeDtypeStruct((n, d), BF16),
        compiler_params=_params("arbitrary"),
        name="modulate",
    )(x, gain.reshape(1, d), shift, scale)


def _split_bf16(x):
    hi = x.astype(BF16)
    lo = (x - hi.astype(F32)).astype(BF16)
    return hi, lo


def _bf16_bits(x):
    return lax.bitcast_convert_type(x.astype(BF16).astype(F32), jnp.int32)


_UPPER_HALF = -65536


def _pack_bf16_pairs(lo, hi):
    return lax.shift_right_logical(_bf16_bits(lo), 16) | (_bf16_bits(hi) & _UPPER_HALF)


def _unpack_bf16_pairs(words):
    return (lax.bitcast_convert_type(words << 16, F32),
            lax.bitcast_convert_type(words & _UPPER_HALF, F32))


def _packed_row_words(d):
    return d // 2 + d // 8


PACK_COLS = 1024
PACK_PAD = 128


def _pack_cols(d):
    return min(PACK_COLS, d)


def _pack_tile_words(d):
    return _pack_cols(d) // 2 + PACK_PAD


def _router_body(x_ref, gain_ref, sh_ref, sc_ref, wr_ref, o_ref, p_ref, s_ref):
    h = _modulated(x_ref[...], gain_ref[...], sh_ref[...], sc_ref[...])
    o_ref[...] = h.astype(BF16)
    half = h.shape[1] // 2
    packed = _pack_bf16_pairs(h[:, :half], h[:, half:])
    is_pad_block = pl.program_id(0) == pl.num_programs(0) - 1
    p_ref[:, :half] = jnp.where(is_pad_block, 0, packed)
    p_ref[:, half:] = jnp.zeros((h.shape[0], p_ref.shape[1] - half), jnp.int32)
    h_hi, h_lo = _split_bf16(h)
    w_hi, w_lo = _split_bf16(wr_ref[...])
    logits = (jnp.dot(h_hi, w_hi, preferred_element_type=F32)
              + jnp.dot(h_lo, w_hi, preferred_element_type=F32)
              + jnp.dot(h_hi, w_lo, preferred_element_type=F32))
    s_ref[...] = jax.nn.sigmoid(logits)


def modulate_and_route(x, gain, shift, scale, w_router):
    n, d = x.shape
    e = w_router.shape[1]
    tm = ROW_BLOCK
    n_rb = n // tm
    rb = lambda i: jnp.minimum(i, n_rb - 1)
    grp = pl.BlockSpec((None, 1, d), lambda i: (_group_of_block(rb(i), tm), 0, 0))
    return pl.pallas_call(
        _router_body,
        grid=(n_rb + 1,),
        in_specs=[pl.BlockSpec((tm, d), lambda i: (rb(i), 0)),
                  pl.BlockSpec((1, d), lambda i: (0, 0)),
                  grp, grp,
                  pl.BlockSpec((d, e), lambda i: (0, 0))],
        out_specs=[pl.BlockSpec((tm, d), lambda i: (rb(i), 0)),
                   pl.BlockSpec((tm, _packed_row_words(d)), lambda i: (i, 0)),
                   pl.BlockSpec((tm, e), lambda i: (rb(i), 0))],
        out_shape=[jax.ShapeDtypeStruct((n, d), BF16),
                   jax.ShapeDtypeStruct((n + tm, _packed_row_words(d)), jnp.int32),
                   jax.ShapeDtypeStruct((n, e), F32)],
        compiler_params=_params("arbitrary"),
        name="modulate_route",
    )(x, gain.reshape(1, d), shift, scale, w_router)


def _mm_body(*refs, n_w, n_extra, epilogue):
    a_ref = refs[0]
    w_refs = refs[1:1 + n_w]
    extra = refs[1 + n_w:1 + n_w + n_extra]
    o_ref = refs[1 + n_w + n_extra]
    w_bf = refs[2 + n_w + n_extra:]

    @pl.when(pl.program_id(1) == 0)
    def _():
        for w_ref, w_s in zip(w_refs, w_bf):
            w_s[...] = w_ref[...].astype(BF16)

    a = a_ref[...]
    accs = [jnp.dot(a, w_s[...], preferred_element_type=F32) for w_s in w_bf]
    o_ref[...] = epilogue(*accs, *[r[...] for r in extra]).astype(o_ref.dtype)


def _stacked(w, *lead):
    flat = 0
    for size, idx in zip(w.shape[:len(lead)], lead):
        flat = flat * size + idx
    return w.reshape((-1,) + w.shape[len(lead):]), flat


def matmul(name, a, ws, epilogue=None, extra=(), out_dtype=F32, tm=None, tn=None):
    m, k = a.shape
    n = ws[0][0].shape[2]
    tm = min(tm or MM_TM, m)
    tn = min(tn or MM_TN, n)
    assert m % tm == 0 and n % tn == 0, (name, m, n, tm, tn)
    if epilogue is None:
        epilogue = lambda acc: acc
    body = functools.partial(_mm_body, n_w=len(ws), n_extra=len(extra), epilogue=epilogue)

    def w_spec(idx):
        return pl.BlockSpec((None, k, tn), lambda j, i: (idx, 0, j))

    return pl.pallas_call(
        body,
        grid=(n // tn, m // tm),
        in_specs=([pl.BlockSpec((tm, k), lambda j, i: (i, 0))]
                  + [w_spec(idx) for _, idx in ws]
                  + [spec for _, spec in extra]),
        out_specs=pl.BlockSpec((tm, tn), lambda j, i: (i, j)),
        out_shape=jax.ShapeDtypeStruct((m, n), out_dtype),
        scratch_shapes=[pltpu.VMEM((k, tn), BF16) for _ in ws],
        compiler_params=_params("arbitrary", "arbitrary"),
        name=name,
    )(a, *[w for w, _ in ws], *[arr for arr, _ in extra])


def _row_vec_spec(tn):
    return pl.BlockSpec((1, tn), lambda j, i: (0, j))


def _gate_spec(tm, tn):
    return pl.BlockSpec((None, 1, tn), lambda j, i: (_group_of_block(i, tm), 0, j))


def _tile_spec(tm, tn):
    return pl.BlockSpec((tm, tn), lambda j, i: (i, j))


def _rms_head(x, gain):
    return x * lax.rsqrt(jnp.mean(x * x, axis=-1, keepdims=True) + NORM_EPS) * gain


def _rope(x, cos_f, sin_f):
    return x * cos_f + pltpu.roll(x, HEAD_DIM // 2, 1) * sin_f


def _attn_ctx_body(q_ref, k_ref, v_ref, qg_ref, kg_ref, o_ref, kc_ref, vc_ref):
    kn = _rms_head(k_ref[...], kg_ref[...])
    kc_ref[...] = kn
    v = v_ref[...]
    vc_ref[...] = v
    kb = kn.astype(BF16)
    vb = v.astype(BF16)
    scale = HEAD_DIM ** -0.5
    for h in range(N_HEADS // N_KV_HEADS):
        sl = slice(h * HEAD_DIM, (h + 1) * HEAD_DIM)
        qh = _rms_head(q_ref[:, sl], qg_ref[...]).astype(BF16)
        s = lax.dot_general(qh, kb, (((1,), (1,)), ((), ())), preferred_element_type=F32) * scale
        p = jnp.exp(s - jnp.max(s, axis=-1, keepdims=True))
        l = jnp.sum(p, axis=-1, keepdims=True)
        o = jnp.dot(p.astype(BF16), vb, preferred_element_type=F32) / l
        o_ref[:, sl] = o.astype(o_ref.dtype)


def attention_context(qkv, q_gain, k_gain):
    n = _n_ctx()
    g = N_HEADS // N_KV_HEADS
    hd = HEAD_DIM
    return pl.pallas_call(
        _attn_ctx_body,
        grid=(BATCH, N_KV_HEADS),
        in_specs=[pl.BlockSpec((SEQ, g * hd), lambda b, kh: (b, kh)),
                  pl.BlockSpec((SEQ, hd), lambda b, kh: (b, N_HEADS + kh)),
                  pl.BlockSpec((SEQ, hd), lambda b, kh: (b, N_HEADS + N_KV_HEADS + kh)),
                  pl.BlockSpec((1, hd), lambda b, kh: (0, 0)),
                  pl.BlockSpec((1, hd), lambda b, kh: (0, 0))],
        out_specs=[pl.BlockSpec((SEQ, g * hd), lambda b, kh: (b, kh)),
                   pl.BlockSpec((SEQ, hd), lambda b, kh: (b, kh)),
                   pl.BlockSpec((SEQ, hd), lambda b, kh: (b, kh))],
        out_shape=[jax.ShapeDtypeStruct((n, N_HEADS * hd), BF16),
                   jax.ShapeDtypeStruct((_n_ctx(), N_KV_HEADS * hd), F32),
                   jax.ShapeDtypeStruct((_n_ctx(), N_KV_HEADS * hd), F32)],
        compiler_params=_params("arbitrary", "arbitrary"),
        name="attn_ctx",
    )(qkv, qkv, qkv, q_gain.reshape(1, hd), k_gain.reshape(1, hd))


def _attn_lat_body(q_ref, k_ref, v_ref, ck_ref, cv_ref, cosq_ref, sinq_ref, cosk_ref, sink_ref,
                   qg_ref, kg_ref, o_ref):
    kn = _rope(_rms_head(k_ref[...], kg_ref[...]), cosk_ref[...], sink_ref[...]).astype(BF16)
    vb = v_ref[...].astype(BF16)
    ckb = ck_ref[...].astype(BF16)
    cvb = cv_ref[...].astype(BF16)
    scale = HEAD_DIM ** -0.5
    nt = (((1,), (1,)), ((), ()))
    for h in range(N_HEADS // N_KV_HEADS):
        sl = slice(h * HEAD_DIM, (h + 1) * HEAD_DIM)
        qh = _rope(_rms_head(q_ref[:, sl], qg_ref[...]), cosq_ref[...], sinq_ref[...]).astype(BF16)
        s_c = lax.dot_general(qh, ckb, nt, preferred_element_type=F32) * scale
        s_l = lax.dot_general(qh, kn, nt, preferred_element_type=F32) * scale
        m = jnp.maximum(jnp.max(s_c, axis=-1, keepdims=True), jnp.max(s_l, axis=-1, keepdims=True))
        p_c = jnp.exp(s_c - m)
        p_l = jnp.exp(s_l - m)
        l = jnp.sum(p_c, axis=-1, keepdims=True) + jnp.sum(p_l, axis=-1, keepdims=True)
        o = (jnp.dot(p_c.astype(BF16), cvb, preferred_element_type=F32)
             + jnp.dot(p_l.astype(BF16), vb, preferred_element_type=F32)) / l
        o_ref[:, sl] = o.astype(o_ref.dtype)


def attention_latent(qkv, cache_k, cache_v, cos_f, sin_f, q_gain, k_gain):
    g = N_HEADS // N_KV_HEADS
    hd = HEAD_DIM
    tq = min(ATTN_TQ, DEC_SEQ)
    nq = DEC_SEQ // tq
    q_row0 = _n_ctx() // tq
    k_row0 = _n_ctx() // DEC_SEQ
    q_map = lambda b, kh, qi: (q_row0 + b * nq + qi, kh)
    return pl.pallas_call(
        _attn_lat_body,
        grid=(DEC_BATCH, N_KV_HEADS, nq),
        in_specs=[pl.BlockSpec((tq, g * hd), q_map),
                  pl.BlockSpec((DEC_SEQ, hd), lambda b, kh, qi: (k_row0 + b, N_HEADS + kh)),
                  pl.BlockSpec((DEC_SEQ, hd), lambda b, kh, qi: (k_row0 + b, N_HEADS + N_KV_HEADS + kh)),
                  pl.BlockSpec((None, None, PAST_LEN, hd), lambda b, kh, qi: (b, kh, 0, 0)),
                  pl.BlockSpec((None, None, PAST_LEN, hd), lambda b, kh, qi: (b, kh, 0, 0)),
                  pl.BlockSpec((tq, hd), lambda b, kh, qi: (qi, 0)),
                  pl.BlockSpec((tq, hd), lambda b, kh, qi: (qi, 0)),
                  pl.BlockSpec((DEC_SEQ, hd), lambda b, kh, qi: (0, 0)),
                  pl.BlockSpec((DEC_SEQ, hd), lambda b, kh, qi: (0, 0)),
                  pl.BlockSpec((1, hd), lambda b, kh, qi: (0, 0)),
                  pl.BlockSpec((1, hd), lambda b, kh, qi: (0, 0))],
        out_specs=pl.BlockSpec((tq, g * hd), lambda b, kh, qi: (b * nq + qi, kh)),
        out_shape=jax.ShapeDtypeStruct((DEC_BATCH * DEC_SEQ, N_HEADS * hd), BF16),
        compiler_params=_params("arbitrary", "arbitrary", "arbitrary"),
        name="attn_lat",
    )(qkv, qkv, qkv, cache_k, cache_v, cos_f, sin_f, cos_f, sin_f,
      q_gain.reshape(1, hd), k_gain.reshape(1, hd))


def rope_tables():
    rows = DEC_SEQ // GRID_W
    r_idx, c_idx = jnp.meshgrid(jnp.arange(rows), jnp.arange(GRID_W), indexing="ij")
    r_idx = r_idx.reshape(-1).astype(F32)
    c_idx = c_idx.reshape(-1).astype(F32)
    n_freq = HEAD_DIM // 4
    inv = ROPE_THETA ** (-jnp.arange(n_freq, dtype=F32) / n_freq)
    ang = jnp.concatenate([r_idx[:, None] * inv, c_idx[:, None] * inv], axis=-1)
    cos, sin = jnp.cos(ang), jnp.sin(ang)
    return jnp.concatenate([cos, cos], axis=-1), jnp.concatenate([-sin, sin], axis=-1)


def _mix_body(x_ref, xp_ref, xn_ref, gain_ref, sh_ref, sc_ref, mu_ref, *o_refs):
    i = pl.program_id(0)
    tm = x_ref.shape[0]
    n_ctx_blocks = _n_ctx() // tm
    per_seq = DEC_SEQ // tm
    j = (i - n_ctx_blocks) % per_seq
    is_first = jnp.logical_or(i < n_ctx_blocks, j == 0)
    is_last = jnp.logical_or(i < n_ctx_blocks, j == per_seq - 1)
    gain, sh, sc = gain_ref[...], sh_ref[...], sc_ref[...]
    h = _modulated(x_ref[...], gain, sh, sc)
    h_prev = jnp.where(is_first, 0.0, _modulated(xp_ref[7:8, :], gain, sh, sc))
    h_next = jnp.where(is_last, 0.0, _modulated(xn_ref[0:1, :], gain, sh, sc))
    row = lax.broadcasted_iota(jnp.int32, h.shape, 0)
    prev = jnp.where(row == 0, h_prev, pltpu.roll(h, 1, 0))
    nxt = jnp.where(row == tm - 1, h_next, pltpu.roll(h, tm - 1, 0))
    xx = 0.5 * (prev + nxt) - h
    for n, o_ref in enumerate(o_refs):
        o_ref[...] = (h + xx * mu_ref[n:n + 1, :]).astype(o_ref.dtype)


def rwkv_mix(x, gain, shift, scale, mu):
    n, d = x.shape
    tm = ROW_BLOCK
    rb = tm // 8
    last8 = n // 8 - 1
    return pl.pallas_call(
        _mix_body,
        grid=(n // tm,),
        in_specs=[pl.BlockSpec((tm, d), lambda i: (i, 0)),
                  pl.BlockSpec((8, d), lambda i: (jnp.maximum(i * rb - 1, 0), 0)),
                  pl.BlockSpec((8, d), lambda i: (jnp.minimum((i + 1) * rb, last8), 0)),
                  pl.BlockSpec((1, d), lambda i: (0, 0)),
                  _group_spec(d, tm), _group_spec(d, tm),
                  pl.BlockSpec((mu.shape[0], d), lambda i: (0, 0))],
        out_specs=[pl.BlockSpec((tm, d), lambda i: (i, 0)) for _ in range(6)],
        out_shape=[jax.ShapeDtypeStruct((n, d), BF16) for _ in range(6)],
        compiler_params=_params("arbitrary"),
        name="rwkv_mix",
    )(x, x, x, gain.reshape(1, d), shift, scale, mu)


def _seg_sum(x, ones_bd):
    hi, lo = _split_bf16(x)
    return (jnp.dot(hi, ones_bd, preferred_element_type=F32)
            + jnp.dot(lo, ones_bd, preferred_element_type=F32))


def _scan_body(r_ref, k_ref, v_ref, wl_ref, al_ref, g_ref, kk_ref, ka_ref, rk_ref, lnw_ref, lnb_ref,
               s0_ref, *rest, reverse, add_prev):
    if add_prev:
        prev_ref, o_ref, sfin_ref, s_scr = rest
    else:
        o_ref, sfin_ref, s_scr = rest
    L = SCAN_CHUNK
    R = 2 * L
    hw = RWKV_HEAD
    n_pairs = s_scr.shape[0]
    n_il = min(SCAN_INTERLEAVE, n_pairs)
    step = pl.program_id(0)
    n_chunks = pl.num_programs(0)
    c = (n_chunks - 1 - step) if reverse else step
    ctx_chunks = _n_ctx() // L
    per_ctx, per_lat = SEQ // L, DEC_SEQ // L
    pos = jnp.where(c < ctx_chunks, c % per_ctx, (c - ctx_chunks) % per_lat)
    seq_len = jnp.where(c < ctx_chunks, per_ctx, per_lat)
    first_pos = seq_len - 1 if reverse else 0
    last_pos = 0 if reverse else seq_len - 1

    is_ctx = c < ctx_chunks

    @pl.when(jnp.logical_and(pos == first_pos, is_ctx))
    def _():
        s_scr[...] = jnp.zeros(s_scr.shape, F32)

    @pl.when(jnp.logical_and(pos == first_pos, jnp.logical_not(is_ctx)))
    def _():
        s_scr[...] = s0_ref[...]

    ri = lax.broadcasted_iota(jnp.int32, (L, L), 0)
    ci = lax.broadcasted_iota(jnp.int32, (L, L), 1)
    before = (ci > ri) if reverse else (ci < ri)
    incl_bf = jnp.logical_or(before, ci == ri).astype(BF16)
    r4 = lax.broadcasted_iota(jnp.int32, (2 * R, 2 * R), 0)
    c4 = lax.broadcasted_iota(jnp.int32, (2 * R, 2 * R), 1)
    same_head = ((r4 % R) // L) == ((c4 % R) // L)
    t4, j4 = r4 % L, c4 % L
    before4 = (j4 > t4) if reverse else (j4 < t4)
    mask4 = jnp.logical_and(same_head, jnp.logical_or(before4, jnp.logical_and(r4 >= R, j4 == t4)))
    eye = (lax.broadcasted_iota(jnp.int32, (R, R), 0) == lax.broadcasted_iota(jnp.int32, (R, R), 1)).astype(F32)
    lr = lax.broadcasted_iota(jnp.int32, (LANES, LANES), 0)
    lc = lax.broadcasted_iota(jnp.int32, (LANES, LANES), 1)
    ones_bd = ((lr // hw) == (lc // hw)).astype(BF16)
    head_a = lax.broadcasted_iota(jnp.int32, (L, LANES), 1) < hw
    nt = (((1,), (1,)), ((), ()))
    inv_hw = 1.0 / hw
    Q = range(n_il)

    def stack(x):
        return jnp.concatenate([jnp.where(head_a, x, 0.0), jnp.where(head_a, 0.0, x)], axis=0)

    def mm(x, y):
        return jnp.dot(x.astype(BF16), y.astype(BF16), preferred_element_type=F32)

    def group(gi, carry):
        sls = [pl.ds(pl.multiple_of((gi * n_il + q) * LANES, LANES), LANES) for q in Q]
        s_vk = [s_scr[gi * n_il + q] for q in Q]
        r = [r_ref[:, sl] for sl in sls]
        k = [k_ref[:, sl] for sl in sls]
        v = [v_ref[:, sl] for sl in sls]
        lw = [DECAY_SCALE * jax.nn.sigmoid(wl_ref[:, sl]) for sl in sls]
        ag = [jax.nn.sigmoid(al_ref[:, sl]) for sl in sls]
        kkf = [k[q] * kk_ref[:, sls[q]] for q in Q]
        nrm = [_seg_sum(kkf[q] * kkf[q], ones_bd) for q in Q]
        kk = [kkf[q] / jnp.maximum(jnp.sqrt(nrm[q]), 1e-12) for q in Q]
        kd = [k[q] * (1.0 + (ag[q] - 1.0) * ka_ref[:, sls[q]]) for q in Q]
        cs = []
        for q in Q:
            w_hi = lw[q].astype(BF16)
            w_r1 = lw[q] - w_hi.astype(F32)
            w_mid = w_r1.astype(BF16)
            w_lo = (w_r1 - w_mid.astype(F32)).astype(BF16)
            cs.append(jnp.dot(incl_bf, w_hi, preferred_element_type=F32)
                      + jnp.dot(incl_bf, w_mid, preferred_element_type=F32)
                      + jnp.dot(incl_bf, w_lo, preferred_element_type=F32))
        e_neg = [jnp.exp(-cs[q]) for q in Q]
        ar = [jnp.concatenate([stack(jnp.exp(cs[q] - lw[q]) * -kk[q]), stack(jnp.exp(cs[q]) * r[q])],
                              axis=0).astype(BF16) for q in Q]
        bk = [jnp.concatenate([stack(e_neg[q] * kk[q] * ag[q]), stack(e_neg[q] * kd[q])],
                              axis=0).astype(BF16) for q in Q]
        v_s = [stack(v[q]).astype(BF16) for q in Q]
        gm = [jnp.where(mask4, lax.dot_general(ar[q], bk[q], nt, preferred_element_type=F32), 0.0) for q in Q]
        a_s = [lax.dot_general(ar[q], s_vk[q].astype(BF16), nt, preferred_element_type=F32) for q in Q]
        w0 = [a_s[q][:R] + mm(gm[q][:R, R:], v_s[q]) for q in Q]
        m = [eye + gm[q][:R, :R] for q in Q]
        pw = [gm[q][:R, :R] for q in Q]
        n = 1
        while n < L // 2:
            pw = [mm(pw[q], pw[q]) for q in Q]
            m = [m[q] + mm(m[q], pw[q]) for q in Q]
            n *= 2
        u_s = [mm(m[q], w0[q]) for q in Q]
        uv = [jnp.concatenate([u_s[q], v_s[q].astype(F32)], axis=0) for q in Q]
        y_s = [a_s[q][R:] + mm(gm[q][R:, :], uv[q]) for q in Q]
        y = [y_s[q][:L] + y_s[q][L:] for q in Q]
        upd = [mm(uv[q].T, bk[q]) for q in Q]
        for q in Q:
            c_tot = cs[q][0:1, :] if reverse else cs[q][L - 1:L, :]
            s_scr[gi * n_il + q] = (s_vk[q] + upd[q]) * jnp.exp(c_tot)
        mean = [_seg_sum(y[q], ones_bd) * inv_hw for q in Q]
        dy = [y[q] - mean[q] for q in Q]
        var = [_seg_sum(dy[q] * dy[q], ones_bd) * inv_hw for q in Q]
        bonus = [_seg_sum(r[q] * kd[q] * rk_ref[:, sls[q]], ones_bd) * v[q] for q in Q]
        for q in Q:
            sl = sls[q]
            yn = dy[q] * lax.rsqrt(var[q] + GN_EPS) * lnw_ref[:, sl] + lnb_ref[:, sl]
            out = (yn + bonus[q]) * g_ref[:, sl]
            if add_prev:
                out = out + prev_ref[:, sl].astype(F32)
            o_ref[:, sl] = out.astype(o_ref.dtype)
        return carry

    lax.fori_loop(0, n_pairs // n_il, group, 0)

    @pl.when(jnp.logical_and(pos == last_pos, is_ctx))
    def _():
        for p in range(n_pairs):
            s_pair = s_scr[p]
            sfin_ref[2 * p] = s_pair[:hw, :hw]
            sfin_ref[2 * p + 1] = pltpu.roll(s_pair[hw:, :], hw, 1)[:, :hw]


def wkv_scan_dir(r, k, v, wl, al, g, k_k, k_a, r_k, lnx_w, lnx_b, s0, reverse, prev=None):
    n, d = r.shape
    L = SCAN_CHUNK
    n_chunks = n // L
    n_pairs = d // LANES
    ctx_chunks = _n_ctx() // L

    def cidx(s):
        return (n_chunks - 1 - s) if reverse else s

    def seq_of(s):
        c = cidx(s)
        return jnp.where(c < ctx_chunks, c // (SEQ // L), BATCH + (c - ctx_chunks) // (DEC_SEQ // L))

    tok = pl.BlockSpec((L, d), lambda s: (cidx(s), 0))
    vec = pl.BlockSpec((1, d), lambda s: (0, 0))
    hw = RWKV_HEAD
    st_in = pl.BlockSpec((None, n_pairs, LANES, LANES),
                         lambda s: (jnp.clip(seq_of(s) - BATCH, 0, DEC_BATCH - 1), 0, 0, 0))
    st_out = pl.BlockSpec((None, d // hw, hw, hw), lambda s: (jnp.minimum(seq_of(s), BATCH - 1), 0, 0, 0))
    add_prev = prev is not None
    body = functools.partial(_scan_body, reverse=reverse, add_prev=add_prev)
    args = [r, k, v, wl, al, g, k_k.reshape(1, d), k_a.reshape(1, d), r_k.reshape(1, d),
            lnx_w.reshape(1, d), lnx_b.reshape(1, d), s0]
    in_specs = [tok] * 6 + [vec] * 5 + [st_in]
    if add_prev:
        args.append(prev)
        in_specs.append(tok)
    return pl.pallas_call(
        body,
        grid=(n_chunks,),
        in_specs=in_specs,
        out_specs=[tok, st_out],
        out_shape=[jax.ShapeDtypeStruct((n, d), BF16 if add_prev else F32),
                   jax.ShapeDtypeStruct((BATCH, d // hw, hw, hw), F32)],
        scratch_shapes=[pltpu.VMEM((n_pairs, LANES, LANES), F32)],
        compiler_params=_params("arbitrary"),
        name="wkv_bwd" if reverse else "wkv_fwd",
    )(*args)


def _pair_states(s):
    n_seq, h, hv, hk = s.shape
    s = s.reshape(n_seq, h // 2, 2, hv, hk)
    z = jnp.zeros_like(s[:, :, 0])
    top = jnp.concatenate([s[:, :, 0], z], axis=-1)
    bot = jnp.concatenate([z, s[:, :, 1]], axis=-1)
    return jnp.concatenate([top, bot], axis=-2)


ITEM_BLOCKS = 4


def _store_rows(o_ref, value):
    rows = value.shape[0]
    o_ref[:rows, :] = value.astype(o_ref.dtype)
    if rows < o_ref.shape[0]:
        o_ref[rows:, :] = jnp.zeros((o_ref.shape[0] - rows, o_ref.shape[1]), o_ref.dtype)


def _expert_up_body(ie_ref, ib_ref, inb_ref, last_ref, *refs):
    del ie_ref, ib_ref, last_ref
    x_refs = refs[:ITEM_BLOCKS]
    wg_ref, wu_ref, o_ref = refs[ITEM_BLOCKS:]
    nb = inb_ref[pl.program_id(0)]
    half = x_refs[0].shape[1]
    for n_sub in range(1, ITEM_BLOCKS + 1):
        @pl.when(nb == n_sub)
        def _(n_sub=n_sub):
            xp = jnp.concatenate([x_refs[s][...] for s in range(n_sub)], axis=0)
            x_lo, x_hi = [v.astype(BF16) for v in _unpack_bf16_pairs(xp)]
            wg = wg_ref[...].astype(BF16)
            wu = wu_ref[...].astype(BF16)
            gate = (jnp.dot(x_lo, wg[:half], preferred_element_type=F32)
                    + jnp.dot(x_hi, wg[half:], preferred_element_type=F32))
            up = (jnp.dot(x_lo, wu[:half], preferred_element_type=F32)
                  + jnp.dot(x_hi, wu[half:], preferred_element_type=F32))
            _store_rows(o_ref, _silu(gate) * up)


def _item_x_spec(s, bm, half):
    def index(it, f, ie, ib, inb, last):
        return ib[it] + jnp.minimum(s, jnp.maximum(inb[it] - 1, 0)), 0
    return pl.BlockSpec((bm, half), index)


def expert_up(xs, w_gate, w_up, layer, item_e, item_blk, item_nb, last_item, tf=256):
    d = w_gate.shape[2]
    half = d // 2
    n_exp = w_gate.shape[1]
    ff = w_gate.shape[-1]
    w_gate = w_gate.reshape((-1,) + w_gate.shape[2:])
    w_up = w_up.reshape((-1,) + w_up.shape[2:])
    bm = ROW_BLOCK
    tf = min(tf, ff)
    n_f = ff // tf
    n_items = item_e.shape[0]
    item_rows = ITEM_BLOCKS * bm

    def col(it, f, last):
        return jnp.where(it > last[0], n_f - 1, f)

    w_spec = pl.BlockSpec((None, d, tf),
                          lambda it, f, ie, ib, inb, last: (layer * n_exp + ie[it], 0, col(it, f, last)))
    return pl.pallas_call(
        _expert_up_body,
        grid_spec=pltpu.PrefetchScalarGridSpec(
            num_scalar_prefetch=4,
            grid=(n_items, n_f),
            in_specs=[_item_x_spec(s, bm, half) for s in range(ITEM_BLOCKS)] + [w_spec, w_spec],
            out_specs=pl.BlockSpec((item_rows, tf),
                                   lambda it, f, ie, ib, inb, last: (jnp.minimum(it, last[0]), col(it, f, last)))),
        out_shape=jax.ShapeDtypeStruct((n_items * item_rows, ff), BF16),
        compiler_params=_params("arbitrary", "arbitrary"),
        name="expert_up",
    )(item_e, item_blk, item_nb, last_item, *([xs] * ITEM_BLOCKS), w_gate, w_up)


def _expert_down_body(ie_ref, ib_ref, inb_ref, last_ref, h_ref, wd_ref, o_ref):
    del ie_ref, ib_ref, last_ref
    nb = inb_ref[pl.program_id(1)]
    bm = h_ref.shape[0] // ITEM_BLOCKS
    for n_sub in range(1, ITEM_BLOCKS + 1):
        @pl.when(nb == n_sub)
        def _(n_sub=n_sub):
            rows = n_sub * bm
            y = jnp.dot(h_ref[:rows, :], wd_ref[...].astype(BF16), preferred_element_type=F32)
            pc = _pack_cols(y.shape[1])
            tw = pc // 2 + PACK_PAD
            for t in range(y.shape[1] // pc):
                o_ref[:rows, t * tw:t * tw + pc // 2] = _pack_bf16_pairs(
                    y[:, t * pc:t * pc + pc // 2], y[:, t * pc + pc // 2:(t + 1) * pc])
                o_ref[:rows, t * tw + pc // 2:(t + 1) * tw] = jnp.zeros((rows, PACK_PAD), jnp.int32)
            if rows < o_ref.shape[0]:
                o_ref[rows:, :] = jnp.zeros((o_ref.shape[0] - rows, o_ref.shape[1]), jnp.int32)


def expert_down(hm, w_down, layer, item_e, item_blk, item_nb, last_item, tn=2048):
    n_rows, ff = hm.shape
    d = w_down.shape[-1]
    n_exp = w_down.shape[1]
    w_down = w_down.reshape((-1,) + w_down.shape[2:])
    tn = min(tn, d)
    n_items = item_e.shape[0]
    item_rows = n_rows // n_items
    out_tn = tn // _pack_cols(d) * _pack_tile_words(d)
    return pl.pallas_call(
        _expert_down_body,
        grid_spec=pltpu.PrefetchScalarGridSpec(
            num_scalar_prefetch=4,
            grid=(d // tn, n_items),
            in_specs=[pl.BlockSpec((item_rows, ff), lambda j, it, ie, ib, inb, last: (jnp.minimum(it, last[0]), 0)),
                      pl.BlockSpec((None, ff, tn), lambda j, it, ie, ib, inb, last: (layer * n_exp + ie[it], 0, j))],
            out_specs=pl.BlockSpec((item_rows, out_tn), lambda j, it, ie, ib, inb, last: (jnp.minimum(it, last[0]), j))),
        out_shape=jax.ShapeDtypeStruct((n_rows, d // tn * out_tn), jnp.int32),
        compiler_params=_params("arbitrary", "arbitrary"),
        name="expert_down",
    )(item_e, item_blk, item_nb, last_item, hm, w_down)


def _moe_out_body(hs_ref, w_ref, *refs):
    n_k = len(refs) - 5
    g_refs = refs[:n_k]
    gw_ref, res_ref, gate_ref, o_ref, w_bf = refs[n_k:]

    @pl.when(pl.program_id(1) == 0)
    def _():
        w_bf[...] = w_ref[...].astype(BF16)

    shared = jnp.dot(hs_ref[...], w_bf[...], preferred_element_type=F32)
    tm, pc = shared.shape
    half = pc // 2
    gw = gw_ref[...]
    lo = jnp.zeros((tm, half), F32)
    hi = jnp.zeros((tm, half), F32)
    for k, g_ref in enumerate(g_refs):
        w_lo, w_hi = _unpack_bf16_pairs(g_ref[:, :half])
        lo = lo + gw[:, k:k + 1] * w_lo
        hi = hi + gw[:, k:k + 1] * w_hi
    res = res_ref[...]
    gate = gate_ref[...]
    o_ref[:, :half] = res[:, :half] + gate[:, :half] * (shared[:, :half] + lo)
    o_ref[:, half:] = res[:, half:] + gate[:, half:] * (shared[:, half:] + hi)


def moe_out(hs, w_down, gathered, gw, resid, gate):
    n, ff = hs.shape
    w_stack, w_idx = w_down
    d = w_stack.shape[2]
    n_k = gw.shape[1]
    tm = ROW_BLOCK
    pc = _pack_cols(d)
    tw = _pack_tile_words(d)

    def g_spec(k):
        return pl.BlockSpec((tm, tw), lambda j, i: (k * (n // tm) + i, j))

    return pl.pallas_call(
        _moe_out_body,
        grid=(d // pc, n // tm),
        in_specs=[pl.BlockSpec((tm, ff), lambda j, i: (i, 0)),
                  pl.BlockSpec((None, ff, pc), lambda j, i: (w_idx, 0, j))]
                 + [g_spec(k) for k in range(n_k)]
                 + [pl.BlockSpec((tm, n_k), lambda j, i: (i, 0)),
                  pl.BlockSpec((tm, pc), lambda j, i: (i, j)),
                  pl.BlockSpec((None, 1, pc), lambda j, i: (_group_of_block(i, tm), 0, j))],
        out_specs=pl.BlockSpec((tm, pc), lambda j, i: (i, j)),
        out_shape=jax.ShapeDtypeStruct((n, d), F32),
        scratch_shapes=[pltpu.VMEM((ff, pc), BF16)],
        compiler_params=_params("arbitrary", "arbitrary"),
        name="moe_out",
    )(hs, w_stack, *([gathered] * n_k), gw, resid, gate)


def route(scores, b_router):
    n = scores.shape[0]
    bm = ROW_BLOCK
    _, idx = lax.top_k(scores + b_router.astype(F32), TOP_K)
    sel = jnp.take_along_axis(scores, idx, axis=-1)
    gw = sel / jnp.sum(sel, axis=-1, keepdims=True) * ROUTED_SCALE
    a = n * TOP_K
    n_blocks = a // bm + N_EXPERTS
    e_flat = idx.reshape(-1)
    onehot = (e_flat[:, None] == jnp.arange(N_EXPERTS)[None, :]).astype(jnp.int32)
    rank = jnp.take_along_axis(jnp.cumsum(onehot, axis=0), e_flat[:, None], axis=1)[:, 0] - 1
    counts = jnp.sum(onehot, axis=0)
    padded = (counts + bm - 1) // bm * bm
    pad_end = jnp.cumsum(padded)
    pad_start = pad_end - padded
    dest = (pad_start[e_flat] + rank).astype(jnp.int32)
    tok_flat = jnp.arange(a, dtype=jnp.int32) // TOP_K
    slot_tok = jnp.full((n_blocks * bm,), n, jnp.int32).at[dest].set(tok_flat)
    ib = ITEM_BLOCKS
    n_items = n_blocks // ib + N_EXPERTS
    e_blocks = padded // bm
    e_items = (e_blocks + ib - 1) // ib
    item_end = jnp.cumsum(e_items)
    item_begin = item_end - e_items
    last_item = (item_end[-1:] - 1).astype(jnp.int32)
    it = jnp.minimum(jnp.arange(n_items), last_item[0])
    item_e = jnp.minimum(jnp.searchsorted(item_end, it, side="right"), N_EXPERTS - 1).astype(jnp.int32)
    local = it - item_begin[item_e]
    item_blk = (pad_start[item_e] // bm + ib * local).astype(jnp.int32)
    item_nb = jnp.where(jnp.arange(n_items) <= last_item[0],
                        jnp.clip(e_blocks[item_e] - ib * local, 0, ib), 0).astype(jnp.int32)
    item_rows = ib * bm
    out_row = ((item_begin[e_flat] + rank // item_rows) * item_rows + rank % item_rows).astype(jnp.int32)
    return gw, out_row.reshape(n, TOP_K), slot_tok, item_e, item_blk, item_nb, last_item


def kernel(x_prompt, x_sample, c, cache_k, cache_v, state_wkv, c_ctx, w_ada, b_ada, norm_mix, norm_ffn, attn_w_qkv, attn_w_o, attn_q_gain, attn_k_gain, rw_mu, rw_w_r, rw_w_k, rw_w_v, rw_w_o, rw_w0, rw_w1, rw_w2, rw_a0, rw_a1, rw_a2, rw_g1, rw_g2, rw_k_k, rw_k_a, rw_r_k, rw_lnx_w, rw_lnx_b, moe_w_router, moe_b_router, moe_w_gate, moe_w_up, moe_w_down, sh_w_gate, sh_w_up, sh_w_down):
    d = D_MODEL
    n_ctx, n = _n_ctx(), _n_tok()
    n_grp = 1 + DEC_BATCH
    x = jnp.concatenate([x_prompt.reshape(n_ctx, d), x_sample.reshape(n - n_ctx, d)], axis=0)

    cvec8 = jnp.zeros((8, d), F32).at[0].set(c_ctx).at[1:n_grp].set(c)
    mods = ada_all(cvec8, w_ada, b_ada)

    def mod(layer, m):
        return mods[layer, :n_grp, m * d:(m + 1) * d].reshape(n_grp, 1, d)

    tm, tn = min(MM_TM, n), min(MM_TN, d)
    resid_gate = lambda acc, res, gate: res + gate * acc

    new_k = new_v = new_s = None
    for i in range(DEPTH):
        j = i // 2
        sh1, sc1, g1, sh2, sc2, g2 = [mod(i, m) for m in range(N_MOD)]
        if i % 2 == 0:
            h = modulate(x, norm_mix[i], sh1, sc1)
            qkv = matmul("qkv", h, [_stacked(attn_w_qkv, j)])
            o_ctx, kc, vc = attention_context(qkv, attn_q_gain[j], attn_k_gain[j])
            cos_f, sin_f = rope_tables()
            o_lat = attention_latent(qkv, jnp.swapaxes(cache_k[:, j], 1, 2), jnp.swapaxes(cache_v[:, j], 1, 2),
                                     cos_f, sin_f, attn_q_gain[j], attn_k_gain[j])
            o = jnp.concatenate([o_ctx, o_lat], axis=0)
            new_k = kc.reshape(BATCH, 1, SEQ, N_KV_HEADS, HEAD_DIM)
            new_v = vc.reshape(BATCH, 1, SEQ, N_KV_HEADS, HEAD_DIM)
            w_o = _stacked(attn_w_o, j)
        else:
            xr, xw, xk, xv, xa, xg = rwkv_mix(x, norm_mix[i], sh1, sc1, rw_mu[j])
            r = matmul("rwkv_r", xr, [_stacked(rw_w_r, j)])
            k = matmul("rwkv_k", xk, [_stacked(rw_w_k, j)])
            v = matmul("rwkv_v", xv, [_stacked(rw_w_v, j)])
            lw = DECAY_LORA
            la = AAA_LORA
            lg = GATE_LORA
            lgp = -(-lg // LANES) * LANES
            w_mid = matmul("decay_lora_in", xw, [(jnp.concatenate([rw_w1[j, 0], rw_w1[j, 1]], axis=1)[None], 0)],
                           epilogue=jnp.tanh, out_dtype=BF16)
            a_mid = matmul("rate_lora_in", xa, [(jnp.concatenate([rw_a1[j, 0], rw_a1[j, 1]], axis=1)[None], 0)],
                           out_dtype=BF16)
            g1p = jnp.pad(rw_g1[j], ((0, 0), (0, 0), (0, lgp - lg)))
            g2p = jnp.pad(rw_g2[j], ((0, 0), (0, lgp - lg), (0, 0)))
            g_mid = matmul("gate_lora_in", xg, [(jnp.concatenate([g1p[0], g1p[1]], axis=1)[None], 0)],
                           epilogue=jax.nn.sigmoid, out_dtype=BF16)
            add_vec = lambda acc, vec: acc + vec
            o = None
            finals = []
            for dr in range(2):
                wl = matmul("decay_lora_out", w_mid[:, dr * lw:(dr + 1) * lw], [_stacked(rw_w2, j, dr)],
                            epilogue=add_vec, extra=[(rw_w0[j, dr].reshape(1, d), _row_vec_spec(tn))])
                al = matmul("rate_lora_out", a_mid[:, dr * la:(dr + 1) * la], [_stacked(rw_a2, j, dr)],
                            epilogue=add_vec, extra=[(rw_a0[j, dr].reshape(1, d), _row_vec_spec(tn))])
                gg = matmul("gate_lora_out", g_mid[:, dr * lgp:(dr + 1) * lgp], [(g2p, dr)])
                o, s_fin = wkv_scan_dir(r, k, v, wl, al, gg, rw_k_k[j], rw_k_a[j], rw_r_k[j, dr],
                                        rw_lnx_w[j], rw_lnx_b[j], _pair_states(state_wkv[:, j, dr]),
                                        reverse=(dr == 1), prev=o)
                finals.append(s_fin)
            new_s = jnp.stack(finals, axis=1)[:, None]
            w_o = _stacked(rw_w_o, j)
        x = matmul("mixer_out", o, [w_o], epilogue=resid_gate,
                   extra=[(x, _tile_spec(tm, tn)), (g1, _gate_spec(tm, tn))])

        h2, h2_packed, scores = modulate_and_route(x, norm_ffn[i], sh2, sc2, moe_w_router[i])
        gw, out_row, slot_tok, *items = route(scores, moe_b_router[i])
        xs = jnp.take(h2_packed, slot_tok, axis=0, mode="clip")
        hs = matmul("shared_up", h2, [_stacked(sh_w_gate, i), _stacked(sh_w_up, i)],
                    epilogue=lambda a, b: _silu(a) * b, out_dtype=BF16, tn=256)
        xs, hs = lax.optimization_barrier((xs, hs))
        hm = expert_up(xs, moe_w_gate, moe_w_up, i, *items)
        yb = expert_down(hm, moe_w_down, i, *items)
        gathered = jnp.take(yb, out_row.T.reshape(-1), axis=0, mode="clip")
        x = moe_out(hs, _stacked(sh_w_down, i), gathered, gw, x, g2)

    y_prompt = x[:n_ctx].reshape(BATCH, SEQ, d)
    y_sample = x[n_ctx:].reshape(DEC_BATCH, DEC_SEQ, d)
    return (y_prompt, y_sample, new_k, new_v, new_s)
```
